```python
import jax, jax.numpy as jnp
from jax import lax
import numpy as np

D_MODEL = 1024
BATCH = 8
SEQ = 2048
DEPTH = 1
DEC_BATCH = 128
DEC_SEQ = 4
PAST_LEN = 16384
PAGE_SIZE = 128

D_MIX = D_MODEL
D_CONV = D_MIX // 2
CONV_HEADS = 8
D_POOL = D_MIX - D_CONV
POOL_WINDOWS = (2, 4, 8, 16)
N_POOL_GROUPS = len(POOL_WINDOWS)
POOL_GC = D_POOL // N_POOL_GROUPS
POOL_MAX = max(POOL_WINDOWS)
CONV_W = 3
D_IN = 4 * D_CONV + 2 * D_POOL
EPS = 1e-6

kernel_name = "hymba_conv_pool_decoder_step"


def _rmsnorm(x, g):
    xf = x.astype(jnp.float32)
    r = lax.rsqrt(jnp.mean(xf * xf, axis=-1, keepdims=True) + EPS)
    return (xf * r).astype(x.dtype) * g


def _mixer(h, conv_buf, pool_buf, n_past, w_in, conv_w, pool_w, pool_scale, w_out):
    bsz, s, _ = h.shape
    proj = jnp.einsum('bsd,de->bse', h, w_in)
    b_g, c_g, u, z_a, v, z_b = jnp.split(
        proj, [D_CONV, 2 * D_CONV, 3 * D_CONV, 4 * D_CONV, 4 * D_CONV + D_POOL], axis=-1)

    cu = c_g * u
    up = jnp.concatenate([conv_buf.astype(cu.dtype), cu], axis=1)
    yc = conv_w[0] * up[:, 0:s]
    for k in range(1, CONV_W):
        yc = yc + conv_w[k] * up[:, k:k + s]
    y_a = b_g * yc * jax.nn.silu(z_a)
    new_conv = up[:, -(CONV_W - 1):]

    vp = jnp.concatenate([pool_buf.astype(v.dtype), v], axis=1)
    vpf = vp.astype(jnp.float32)
    cs = jnp.concatenate([jnp.zeros((bsz, 1, D_POOL), jnp.float32),
                          jnp.cumsum(vpf, axis=1)], axis=1)
    cs_end = cs[:, POOL_MAX:POOL_MAX + s]
    pos = n_past + jnp.arange(s, dtype=jnp.int32) + 1
    means = []
    for gi, w in enumerate(POOL_WINDOWS):
        sl = slice(gi * POOL_GC, (gi + 1) * POOL_GC)
        wsum = cs_end[..., sl] - cs[:, POOL_MAX - w:POOL_MAX - w + s, sl]
        cnt = jnp.minimum(pos, w).astype(jnp.float32)[None, :, None]
        means.append(wsum / cnt)
    mean = jnp.concatenate(means, axis=-1)
    pooled = (mean - v.astype(jnp.float32)).astype(v.dtype)
    pooled = pooled.reshape(bsz, s, N_POOL_GROUPS, POOL_GC)
    pooled = jnp.einsum('bsgc,gcd->bsgd', pooled, pool_w).reshape(bsz, s, D_POOL)
    y_b = pooled * pool_scale * jax.nn.silu(z_b)
    new_pool = vp[:, -(POOL_MAX - 1):]

    y = jnp.einsum('bse,ed->bsd', jnp.concatenate([y_a, y_b], axis=-1), w_out)
    return y, new_conv, new_pool


def setup_inputs(seed: int = 0) -> dict:
    key = jax.random.key(seed)
    ks = jax.random.split(key, 12)
    f32 = jnp.float32
    x_prompt = jax.random.normal(ks[0], (BATCH, SEQ, D_MODEL), f32)
    x_sample = jax.random.normal(ks[1], (DEC_BATCH, DEC_SEQ, D_MODEL), f32)
    state_conv = jax.random.normal(ks[2], (DEPTH, DEC_BATCH, CONV_W - 1, D_CONV), f32)
    state_pool = jax.random.normal(ks[3], (DEPTH, DEC_BATCH, POOL_MAX - 1, D_POOL), f32)
    norm_g = 1.0 + 0.02 * jax.random.normal(ks[4], (DEPTH, D_MODEL), f32)
    w_in = jax.random.normal(ks[5], (DEPTH, D_MODEL, D_IN), f32) * D_MODEL ** -0.5
    conv_w = jax.random.normal(ks[6], (DEPTH, CONV_W, D_CONV), f32) * CONV_W ** -0.5
    pool_w = jax.random.normal(ks[7], (DEPTH, N_POOL_GROUPS, POOL_GC, POOL_GC), f32) * POOL_GC ** -0.5
    pool_scale = 1.0 + 0.02 * jax.random.normal(ks[8], (DEPTH, D_POOL), f32)
    w_out = jax.random.normal(ks[9], (DEPTH, D_MIX, D_MODEL), f32) * D_MIX ** -0.5
    final_g = 1.0 + 0.02 * jax.random.normal(ks[10], (D_MODEL,), f32)
    return {"x_prompt": x_prompt, "x_sample": x_sample,
            "state_conv": state_conv, "state_pool": state_pool,
            "norm_g": norm_g, "w_in": w_in, "conv_w": conv_w, "pool_w": pool_w,
            "pool_scale": pool_scale, "w_out": w_out, "final_g": final_g}


def reference(x_prompt, x_sample, state_conv, state_pool, norm_g, w_in, conv_w, pool_w,
              pool_scale, w_out, final_g):
    hp, hs = x_prompt, x_sample
    zero_conv = jnp.zeros((BATCH, CONV_W - 1, D_CONV), x_prompt.dtype)
    zero_pool = jnp.zeros((BATCH, POOL_MAX - 1, D_POOL), x_prompt.dtype)
    conv_p, pool_p, conv_s, pool_s = [], [], [], []
    for l in range(DEPTH):
        yp, ncp, npp = _mixer(_rmsnorm(hp, norm_g[l]), zero_conv, zero_pool, 0,
                              w_in[l], conv_w[l], pool_w[l], pool_scale[l], w_out[l])
        ys, ncs, nps = _mixer(_rmsnorm(hs, norm_g[l]), state_conv[l], state_pool[l], PAST_LEN,
                              w_in[l], conv_w[l], pool_w[l], pool_scale[l], w_out[l])
        hp = hp + yp
        hs = hs + ys
        conv_p.append(ncp); pool_p.append(npp); conv_s.append(ncs); pool_s.append(nps)
    y_prompt = _rmsnorm(hp, final_g)
    y_sample = _rmsnorm(hs, final_g)
    new_state_conv_p = jnp.stack(conv_p, axis=0)
    new_state_pool_p = jnp.stack(pool_p, axis=0)
    new_state_conv_s = jnp.stack(conv_s, axis=0)
    new_state_pool_s = jnp.stack(pool_s, axis=0)
    return (y_prompt, y_sample, new_state_conv_p, new_state_pool_p, new_state_conv_s, new_state_pool_s)
```

```python
import functools

import jax
import jax.numpy as jnp
from jax import lax
from jax.experimental import pallas as pl
from jax.experimental.pallas import tpu as pltpu

D_MODEL = 1024
D_CONV = 512
D_POOL = 512
POOL_WINDOWS = (2, 4, 8, 16)
POOL_GC = D_POOL // len(POOL_WINDOWS)
POOL_MAX = max(POOL_WINDOWS)
CONV_W = 3
D_IN = 4 * D_CONV + 2 * D_POOL
EPS = 1e-6

SEQ_TILE = 512
CONV_PAD = 8
POOL_PAD = 16
VMEM_LIMIT_BYTES = 56 * 1024 * 1024

_F32 = jnp.float32
_BF16 = jnp.bfloat16


def _rmsnorm(x, g):
    ms = jnp.mean(x * x, axis=-1, keepdims=True)
    return (x * lax.rsqrt(ms + EPS)) * g


def _silu(z):
    return z * jax.nn.sigmoid(z)


def _split_proj(proj):
    b_g = proj[:, 0:D_CONV]
    c_g = proj[:, D_CONV:2 * D_CONV]
    u = proj[:, 2 * D_CONV:3 * D_CONV]
    z_a = proj[:, 3 * D_CONV:4 * D_CONV]
    v = proj[:, 4 * D_CONV:4 * D_CONV + D_POOL]
    z_b = proj[:, 4 * D_CONV + D_POOL:]
    return b_g, c_g, u, z_a, v, z_b


def _pool_project(pooled_bf16, pw_ref):
    outs = []
    for gi in range(len(POOL_WINDOWS)):
        lo = gi * POOL_GC
        outs.append(jnp.dot(pooled_bf16[:, lo:lo + POOL_GC], pw_ref[gi],
                            preferred_element_type=_F32))
    return jnp.concatenate(outs, axis=-1)


def _prompt_kernel(x_ref, ng_ref, win_ref, cw_ref, pw_ref, ps_ref, wout_ref, fg_ref,
                   y_ref, nc_ref, np_ref, cu_buf, v_buf, pooled_buf):
    j = pl.program_id(1)
    ts = SEQ_TILE

    @pl.when(j == 0)
    def _():
        cu_buf[0:CONV_PAD, :] = jnp.zeros((CONV_PAD, D_CONV), _F32)
        v_buf[0:POOL_PAD, :] = jnp.zeros((POOL_PAD, D_POOL), _F32)

    x = x_ref[0]
    h = _rmsnorm(x, ng_ref[...]).astype(_BF16)
    proj = jnp.dot(h, win_ref[...], preferred_element_type=_F32)
    b_g, c_g, u, z_a, v, z_b = _split_proj(proj)

    cu = c_g * u
    cu_buf[CONV_PAD:CONV_PAD + ts, :] = cu
    cw = cw_ref[...]
    yc = cw[0:1, :] * cu_buf[CONV_PAD - 2:CONV_PAD - 2 + ts, :]
    yc = yc + cw[1:2, :] * cu_buf[CONV_PAD - 1:CONV_PAD - 1 + ts, :]
    yc = yc + cw[2:3, :] * cu
    y_a = b_g * yc * _silu(z_a)

    v_buf[POOL_PAD:POOL_PAD + ts, :] = v
    for gi, w in enumerate(POOL_WINDOWS):
        lo = gi * POOL_GC
        vg = v[:, lo:lo + POOL_GC]
        acc = vg
        for k in range(1, w):
            acc = acc + v_buf[POOL_PAD - k:POOL_PAD - k + ts, lo:lo + POOL_GC]
        pooled_buf[:, lo:lo + POOL_GC] = (acc * (1.0 / w) - vg).astype(_BF16)

    @pl.when(j == 0)
    def _():
        n = POOL_MAX
        pos = lax.broadcasted_iota(jnp.int32, (n, POOL_GC), 0) + 1
        for gi, w in enumerate(POOL_WINDOWS):
            lo = gi * POOL_GC
            vg = v_buf[POOL_PAD:POOL_PAD + n, lo:lo + POOL_GC]
            acc = vg
            for k in range(1, w):
                acc = acc + v_buf[POOL_PAD - k:POOL_PAD - k + n, lo:lo + POOL_GC]
            cnt = jnp.minimum(pos, w).astype(_F32)
            pooled_buf[0:n, lo:lo + POOL_GC] = (acc / cnt - vg).astype(_BF16)

    pooled = _pool_project(pooled_buf[...], pw_ref)
    y_b = pooled * ps_ref[...] * _silu(z_b)

    ycat = jnp.concatenate([y_a, y_b], axis=-1).astype(_BF16)
    mix = jnp.dot(ycat, wout_ref[...], preferred_element_type=_F32)
    y_ref[0] = _rmsnorm(x + mix, fg_ref[...])

    @pl.when(j == pl.num_programs(1) - 1)
    def _():
        nc_ref[0, 0] = cu_buf[CONV_PAD + ts - (CONV_W - 1):CONV_PAD + ts, :]
        np_ref[0, 0] = v_buf[POOL_PAD + ts - (POOL_MAX - 1):POOL_PAD + ts, :]

    cu_buf[0:CONV_PAD, :] = cu_buf[ts:ts + CONV_PAD, :]
    v_buf[0:POOL_PAD, :] = v_buf[ts:ts + POOL_PAD, :]


def _sample_kernel(x_ref, sc_ref, sp_ref, ng_ref, win_ref, cw_ref, pw_ref, ps_ref, wout_ref,
                   fg_ref, y_ref, nc_ref, np_ref, *, n_seq, n_tok):
    x = x_ref[...]
    h = _rmsnorm(x, ng_ref[...]).astype(_BF16)
    proj = jnp.dot(h, win_ref[...], preferred_element_type=_F32)
    b_g, c_g, u, z_a, v, z_b = _split_proj(proj)

    def tok(a, t):
        return a[t * n_seq:(t + 1) * n_seq, :]

    cu = c_g * u
    cw = cw_ref[...]
    up = [sc_ref[k] for k in range(CONV_W - 1)] + [tok(cu, t) for t in range(n_tok)]
    yc = []
    for t in range(n_tok):
        acc = cw[0:1, :] * up[t]
        for k in range(1, CONV_W):
            acc = acc + cw[k:k + 1, :] * up[t + k]
        yc.append(acc)
    y_a = b_g * jnp.concatenate(yc, axis=0) * _silu(z_a)
    for k in range(CONV_W - 1):
        nc_ref[k] = up[n_tok + k]

    hist = POOL_MAX - 1
    vp = [sp_ref[k] for k in range(hist)] + [tok(v, t) for t in range(n_tok)]
    pooled_rows = []
    for t in range(n_tok):
        parts = []
        for gi, w in enumerate(POOL_WINDOWS):
            lo = gi * POOL_GC
            vg = vp[hist + t][:, lo:lo + POOL_GC]
            acc = vg
            for k in range(1, w):
                acc = acc + vp[hist + t - k][:, lo:lo + POOL_GC]
            parts.append(acc * (1.0 / w) - vg)
        pooled_rows.append(jnp.concatenate(parts, axis=-1))
    pooled = jnp.concatenate(pooled_rows, axis=0).astype(_BF16)
    for k in range(hist):
        np_ref[k] = vp[n_tok + k]

    y_b = _pool_project(pooled, pw_ref) * ps_ref[...] * _silu(z_b)
    ycat = jnp.concatenate([y_a, y_b], axis=-1).astype(_BF16)
    mix = jnp.dot(ycat, wout_ref[...], preferred_element_type=_F32)
    y_ref[...] = _rmsnorm(x + mix, fg_ref[...])


def _const_spec(shape):
    nd = len(shape)
    return pl.BlockSpec(shape, lambda *_: (0,) * nd, pipeline_mode=pl.Buffered(1))


def _weight_specs():
    return [
        _const_spec((1, D_MODEL)),
        _const_spec((D_MODEL, D_IN)),
        _const_spec((CONV_W, D_CONV)),
        _const_spec((len(POOL_WINDOWS), POOL_GC, POOL_GC)),
        _const_spec((1, D_POOL)),
        _const_spec((D_MODEL, D_MODEL)),
        _const_spec((1, D_MODEL)),
    ]


def _prompt_call(x, weights):
    batch, seq, _ = x.shape
    assert seq % SEQ_TILE == 0
    n_tiles = seq // SEQ_TILE
    return pl.pallas_call(
        _prompt_kernel,
        grid=(batch, n_tiles),
        in_specs=[pl.BlockSpec((1, SEQ_TILE, D_MODEL), lambda b, j: (b, j, 0))] + _weight_specs(),
        out_specs=[
            pl.BlockSpec((1, SEQ_TILE, D_MODEL), lambda b, j: (b, j, 0)),
            pl.BlockSpec((1, 1, CONV_W - 1, D_CONV), lambda b, j: (0, b, 0, 0)),
            pl.BlockSpec((1, 1, POOL_MAX - 1, D_POOL), lambda b, j: (0, b, 0, 0)),
        ],
        out_shape=[
            jax.ShapeDtypeStruct((batch, seq, D_MODEL), _F32),
            jax.ShapeDtypeStruct((1, batch, CONV_W - 1, D_CONV), _F32),
            jax.ShapeDtypeStruct((1, batch, POOL_MAX - 1, D_POOL), _F32),
        ],
        scratch_shapes=[
            pltpu.VMEM((CONV_PAD + SEQ_TILE, D_CONV), _F32),
            pltpu.VMEM((POOL_PAD + SEQ_TILE, D_POOL), _F32),
            pltpu.VMEM((SEQ_TILE, D_POOL), _BF16),
        ],
        compiler_params=pltpu.CompilerParams(
            dimension_semantics=("arbitrary", "arbitrary"),
            vmem_limit_bytes=VMEM_LIMIT_BYTES),
        name="prompt_mixer",
    )(x, *weights)


def _sample_call(x_tm, sc_tm, sp_tm, weights, n_seq, n_tok):
    rows = n_seq * n_tok
    return pl.pallas_call(
        functools.partial(_sample_kernel, n_seq=n_seq, n_tok=n_tok),
        grid=(1,),
        in_specs=[
            _const_spec((rows, D_MODEL)),
            _const_spec((CONV_W - 1, n_seq, D_CONV)),
            _const_spec((POOL_MAX - 1, n_seq, D_POOL)),
        ] + _weight_specs(),
        out_specs=[
            pl.BlockSpec((rows, D_MODEL), lambda i: (0, 0)),
            pl.BlockSpec((CONV_W - 1, n_seq, D_CONV), lambda i: (0, 0, 0)),
            pl.BlockSpec((POOL_MAX - 1, n_seq, D_POOL), lambda i: (0, 0, 0)),
        ],
        out_shape=[
            jax.ShapeDtypeStruct((rows, D_MODEL), _F32),
            jax.ShapeDtypeStruct((CONV_W - 1, n_seq, D_CONV), _F32),
            jax.ShapeDtypeStruct((POOL_MAX - 1, n_seq, D_POOL), _F32),
        ],
        compiler_params=pltpu.CompilerParams(
            dimension_semantics=("arbitrary",),
            vmem_limit_bytes=VMEM_LIMIT_BYTES),
        name="sample_mixer",
    )(x_tm, sc_tm, sp_tm, *weights)


def kernel(x_prompt, x_sample, state_conv, state_pool, norm_g, w_in, conv_w, pool_w, pool_scale,
           w_out, final_g):
    assert norm_g.shape[0] == 1, "single-layer trunk"
    weights = (
        norm_g,
        w_in[0].astype(_BF16),
        conv_w[0],
        pool_w[0].astype(_BF16),
        pool_scale,
        w_out[0].astype(_BF16),
        final_g.reshape(1, D_MODEL),
    )
    y_prompt, nc_p, np_p = _prompt_call(x_prompt, weights)

    n_seq, n_tok, _ = x_sample.shape
    x_tm = jnp.swapaxes(x_sample, 0, 1).reshape(n_tok * n_seq, D_MODEL)
    sc_tm = jnp.swapaxes(state_conv[0], 0, 1)
    sp_tm = jnp.swapaxes(state_pool[0], 0, 1)
    y_tm, nc_tm, np_tm = _sample_call(x_tm, sc_tm, sp_tm, weights, n_seq, n_tok)
    y_sample = jnp.swapaxes(y_tm.reshape(n_tok, n_seq, D_MODEL), 0, 1)
    nc_s = jnp.swapaxes(nc_tm, 0, 1)[None]
    np_s = jnp.swapaxes(np_tm, 0, 1)[None]
    return (y_prompt, y_sample, nc_p, np_p, nc_s, np_s)
```

```python
import functools

import jax
import jax.numpy as jnp
from jax import lax
from jax.experimental import pallas as pl
from jax.experimental.pallas import tpu as pltpu

D_MODEL = 1024
D_CONV = 512
D_POOL = 512
POOL_WINDOWS = (2, 4, 8, 16)
POOL_GC = D_POOL // len(POOL_WINDOWS)
POOL_MAX = max(POOL_WINDOWS)
CONV_W = 3
D_IN = 4 * D_CONV + 2 * D_POOL
EPS = 1e-6

SEQ_TILE = 1024
SUB_ROWS = 512
CONV_PAD = 8
POOL_PAD = 16
VMEM_LIMIT_BYTES = 56 * 1024 * 1024

_F32 = jnp.float32
_BF16 = jnp.bfloat16


def _rmsnorm(x, g):
    ms = jnp.mean(x * x, axis=-1, keepdims=True)
    return (x * lax.rsqrt(ms + EPS)) * g


def _silu(z):
    return z * jax.nn.sigmoid(z)


def _dot(a, b):
    return jnp.dot(a, b, precision=lax.Precision.DEFAULT, preferred_element_type=_F32)


def _split_proj(proj):
    b_g = proj[:, 0:D_CONV]
    c_g = proj[:, D_CONV:2 * D_CONV]
    u = proj[:, 2 * D_CONV:3 * D_CONV]
    z_a = proj[:, 3 * D_CONV:4 * D_CONV]
    v = proj[:, 4 * D_CONV:4 * D_CONV + D_POOL]
    z_b = proj[:, 4 * D_CONV + D_POOL:]
    return b_g, c_g, u, z_a, v, z_b


def _pool_project(pooled_groups, pw_ref):
    outs = [_dot(p, pw_ref[gi]) for gi, p in enumerate(pooled_groups)]
    return jnp.concatenate(outs, axis=-1)


def _prompt_kernel(x_ref, ng_ref, win_ref, cw_ref, pw_ref, ps_ref, wout_ref, fg_ref,
                   y_ref, nc_ref, np_ref, cu_buf, v_buf):
    j = pl.program_id(1)
    ts = SEQ_TILE
    head = POOL_MAX

    @pl.when(j == 0)
    def _():
        cu_buf[0:CONV_PAD, :] = jnp.zeros((CONV_PAD, D_CONV), _F32)
        v_buf[0:POOL_PAD, :] = jnp.zeros((POOL_PAD, D_POOL), _F32)

    cw = cw_ref[...]
    pos = j * ts + 1 + lax.broadcasted_iota(jnp.int32, (head, POOL_GC), 0)

    for s in range(ts // SUB_ROWS):
        r0 = s * SUB_ROWS
        x = x_ref[0, r0:r0 + SUB_ROWS, :]
        h = _rmsnorm(x, ng_ref[...])
        proj = _dot(h, win_ref[...])
        b_g, c_g, u, z_a, v, z_b = _split_proj(proj)

        cu = c_g * u
        c0 = CONV_PAD + r0
        cu_buf[c0:c0 + SUB_ROWS, :] = cu
        yc = cw[0:1, :] * cu_buf[c0 - 2:c0 - 2 + SUB_ROWS, :]
        yc = yc + cw[1:2, :] * cu_buf[c0 - 1:c0 - 1 + SUB_ROWS, :]
        yc = yc + cw[2:3, :] * cu
        y_a = b_g * yc * _silu(z_a)

        p0 = POOL_PAD + r0
        v_buf[p0:p0 + SUB_ROWS, :] = v
        pooled = []
        for gi, w in enumerate(POOL_WINDOWS):
            lo = gi * POOL_GC
            vg = v[:, lo:lo + POOL_GC]
            acc = vg
            for k in range(1, w):
                acc = acc + v_buf[p0 - k:p0 - k + SUB_ROWS, lo:lo + POOL_GC]
            if s == 0:
                cnt = jnp.minimum(pos, w).astype(_F32)
                mean = jnp.concatenate([acc[0:head] / cnt, acc[head:] * (1.0 / w)], axis=0)
            else:
                mean = acc * (1.0 / w)
            pooled.append(mean - vg)
        y_b = _pool_project(pooled, pw_ref) * ps_ref[...] * _silu(z_b)

        ycat = jnp.concatenate([y_a, y_b], axis=-1)
        mix = _dot(ycat, wout_ref[...])
        y_ref[0, r0:r0 + SUB_ROWS, :] = _rmsnorm(x + mix, fg_ref[...])

    @pl.when(j == pl.num_programs(1) - 1)
    def _():
        nc_ref[0, 0] = cu_buf[CONV_PAD + ts - (CONV_W - 1):CONV_PAD + ts, :]
        np_ref[0, 0] = v_buf[POOL_PAD + ts - (POOL_MAX - 1):POOL_PAD + ts, :]

    cu_buf[0:CONV_PAD, :] = cu_buf[ts:ts + CONV_PAD, :]
    v_buf[0:POOL_PAD, :] = v_buf[ts:ts + POOL_PAD, :]


def _sample_kernel(x_ref, sc_ref, sp_ref, ng_ref, win_ref, cw_ref, pw_ref, ps_ref, wout_ref,
                   fg_ref, y_ref, nc_ref, np_ref, *, n_seq, n_tok):
    x = x_ref[...]
    h = _rmsnorm(x, ng_ref[...])
    proj = _dot(h, win_ref[...])
    b_g, c_g, u, z_a, v, z_b = _split_proj(proj)

    def tok(a, t):
        return a[t * n_seq:(t + 1) * n_seq, :]

    cu = c_g * u
    cw = cw_ref[...]
    up = [sc_ref[k] for k in range(CONV_W - 1)] + [tok(cu, t) for t in range(n_tok)]
    yc = []
    for t in range(n_tok):
        acc = cw[0:1, :] * up[t]
        for k in range(1, CONV_W):
            acc = acc + cw[k:k + 1, :] * up[t + k]
        yc.append(acc)
    y_a = b_g * jnp.concatenate(yc, axis=0) * _silu(z_a)
    for k in range(CONV_W - 1):
        nc_ref[k] = up[n_tok + k]

    hist = POOL_MAX - 1
    vp = [sp_ref[k] for k in range(hist)] + [tok(v, t) for t in range(n_tok)]
    pooled = []
    for gi, w in enumerate(POOL_WINDOWS):
        lo = gi * POOL_GC
        rows = []
        for t in range(n_tok):
            vg = vp[hist + t][:, lo:lo + POOL_GC]
            acc = vg
            for k in range(1, w):
                acc = acc + vp[hist + t - k][:, lo:lo + POOL_GC]
            rows.append(acc * (1.0 / w) - vg)
        pooled.append(jnp.concatenate(rows, axis=0))
    for k in range(hist):
        np_ref[k] = vp[n_tok + k]

    y_b = _pool_project(pooled, pw_ref) * ps_ref[...] * _silu(z_b)
    ycat = jnp.concatenate([y_a, y_b], axis=-1)
    mix = _dot(ycat, wout_ref[...])
    y_ref[...] = _rmsnorm(x + mix, fg_ref[...])


def _const_spec(shape):
    nd = len(shape)
    return pl.BlockSpec(shape, lambda *_: (0,) * nd, pipeline_mode=pl.Buffered(1))


def _weight_specs():
    return [
        _const_spec((1, D_MODEL)),
        _const_spec((D_MODEL, D_IN)),
        _const_spec((CONV_W, D_CONV)),
        _const_spec((len(POOL_WINDOWS), POOL_GC, POOL_GC)),
        _const_spec((1, D_POOL)),
        _const_spec((D_MODEL, D_MODEL)),
        _const_spec((1, D_MODEL)),
    ]


def _prompt_call(x, weights):
    batch, seq, _ = x.shape
    assert seq % SEQ_TILE == 0 and SEQ_TILE % SUB_ROWS == 0
    n_tiles = seq // SEQ_TILE
    return pl.pallas_call(
        _prompt_kernel,
        grid=(batch, n_tiles),
        in_specs=[pl.BlockSpec((1, SEQ_TILE, D_MODEL), lambda b, j: (b, j, 0))] + _weight_specs(),
        out_specs=[
            pl.BlockSpec((1, SEQ_TILE, D_MODEL), lambda b, j: (b, j, 0)),
            pl.BlockSpec((1, 1, CONV_W - 1, D_CONV), lambda b, j: (0, b, 0, 0)),
            pl.BlockSpec((1, 1, POOL_MAX - 1, D_POOL), lambda b, j: (0, b, 0, 0)),
        ],
        out_shape=[
            jax.ShapeDtypeStruct((batch, seq, D_MODEL), _F32),
            jax.ShapeDtypeStruct((1, batch, CONV_W - 1, D_CONV), _F32),
            jax.ShapeDtypeStruct((1, batch, POOL_MAX - 1, D_POOL), _F32),
        ],
        scratch_shapes=[
            pltpu.VMEM((CONV_PAD + SEQ_TILE, D_CONV), _F32),
            pltpu.VMEM((POOL_PAD + SEQ_TILE, D_POOL), _F32),
        ],
        compiler_params=pltpu.CompilerParams(
            dimension_semantics=("arbitrary", "arbitrary"),
            vmem_limit_bytes=VMEM_LIMIT_BYTES),
        name="prompt_mixer",
    )(x, *weights)


def _sample_call(x_tm, sc_tm, sp_tm, weights, n_seq, n_tok):
    rows = n_seq * n_tok
    return pl.pallas_call(
        functools.partial(_sample_kernel, n_seq=n_seq, n_tok=n_tok),
        grid=(1,),
        in_specs=[
            _const_spec((rows, D_MODEL)),
            _const_spec((CONV_W - 1, n_seq, D_CONV)),
            _const_spec((POOL_MAX - 1, n_seq, D_POOL)),
        ] + _weight_specs(),
        out_specs=[
            pl.BlockSpec((rows, D_MODEL), lambda i: (0, 0)),
            pl.BlockSpec((CONV_W - 1, n_seq, D_CONV), lambda i: (0, 0, 0)),
            pl.BlockSpec((POOL_MAX - 1, n_seq, D_POOL), lambda i: (0, 0, 0)),
        ],
        out_shape=[
            jax.ShapeDtypeStruct((rows, D_MODEL), _F32),
            jax.ShapeDtypeStruct((CONV_W - 1, n_seq, D_CONV), _F32),
            jax.ShapeDtypeStruct((POOL_MAX - 1, n_seq, D_POOL), _F32),
        ],
        compiler_params=pltpu.CompilerParams(
            dimension_semantics=("arbitrary",),
            vmem_limit_bytes=VMEM_LIMIT_BYTES),
        name="sample_mixer",
    )(x_tm, sc_tm, sp_tm, *weights)


def kernel(x_prompt, x_sample, state_conv, state_pool, norm_g, w_in, conv_w, pool_w, pool_scale,
           w_out, final_g):
    assert norm_g.shape[0] == 1, "single-layer trunk"
    weights = (
        norm_g,
        w_in[0],
        conv_w[0],
        pool_w[0],
        pool_scale,
        w_out[0],
        final_g.reshape(1, D_MODEL),
    )
    y_prompt, nc_p, np_p = _prompt_call(x_prompt, weights)

    n_seq, n_tok, _ = x_sample.shape
    x_tm = jnp.swapaxes(x_sample, 0, 1).reshape(n_tok * n_seq, D_MODEL)
    sc_tm = jnp.swapaxes(state_conv[0], 0, 1)
    sp_tm = jnp.swapaxes(state_pool[0], 0, 1)
    y_tm, nc_tm, np_tm = _sample_call(x_tm, sc_tm, sp_tm, weights, n_seq, n_tok)
    y_sample = jnp.swapaxes(y_tm.reshape(n_tok, n_seq, D_MODEL), 0, 1)
    nc_s = jnp.swapaxes(nc_tm, 0, 1)[None]
    np_s = jnp.swapaxes(np_tm, 0, 1)[None]
    return (y_prompt, y_sample, nc_p, np_p, nc_s, np_s)
```

```python
import jax
import jax.numpy as jnp
from jax import lax
from jax.experimental import pallas as pl
from jax.experimental.pallas import tpu as pltpu

D_MODEL = 1024
D_CONV = 512
D_POOL = 512
POOL_WINDOWS = (2, 4, 8, 16)
POOL_GC = D_POOL // len(POOL_WINDOWS)
POOL_MAX = max(POOL_WINDOWS)
CONV_W = 3
D_IN = 4 * D_CONV + 2 * D_POOL
EPS = 1e-6

SEQ_TILE = 1024
SUB_ROWS = 512
CONV_PAD = 8
POOL_PAD = 16
VMEM_LIMIT_BYTES = 56 * 1024 * 1024

_F32 = jnp.float32

_COL_B, _COL_C, _COL_U, _COL_ZA = 0, D_CONV, 2 * D_CONV, 3 * D_CONV
_COL_V, _COL_ZB = 4 * D_CONV, 4 * D_CONV + D_POOL


def _rmsnorm(x, g):
    ms = jnp.mean(x * x, axis=-1, keepdims=True)
    return (x * lax.rsqrt(ms + EPS)) * g


def _silu(z):
    return z * jax.nn.sigmoid(z)


def _dot(a, b):
    return jnp.dot(a, b, precision=lax.Precision.DEFAULT, preferred_element_type=_F32)


def _split_proj(proj):
    return tuple(proj[:, c:c + D_CONV] for c in (_COL_B, _COL_C, _COL_U, _COL_ZA, _COL_V, _COL_ZB))


def _pool_project(pooled_groups, pw_ref):
    outs = [_dot(p, pw_ref[gi]) for gi, p in enumerate(pooled_groups)]
    return jnp.concatenate(outs, axis=-1)


def _prompt_kernel(x_ref, ng_ref, win_ref, cw_ref, pw_ref, ps_ref, wout_ref, fg_ref,
                   y_ref, nc_ref, np_ref, cu_buf, v_buf):
    j = pl.program_id(1)
    ts = SEQ_TILE
    head = POOL_MAX

    @pl.when(j == 0)
    def _():
        cu_buf[0:CONV_PAD, :] = jnp.zeros((CONV_PAD, D_CONV), _F32)
        v_buf[0:POOL_PAD, :] = jnp.zeros((POOL_PAD, D_POOL), _F32)

    cw = cw_ref[...]
    pos = j * ts + 1 + lax.broadcasted_iota(jnp.int32, (head, POOL_GC), 0)

    for s in range(ts // SUB_ROWS):
        r0 = s * SUB_ROWS
        x = x_ref[0, r0:r0 + SUB_ROWS, :]
        proj = _dot(_rmsnorm(x, ng_ref[...]), win_ref[...])
        b_g, c_g, u, z_a, v, z_b = _split_proj(proj)

        cu = c_g * u
        c0 = CONV_PAD + r0
        cu_buf[c0:c0 + SUB_ROWS, :] = cu
        yc = cw[0:1, :] * cu_buf[c0 - 2:c0 - 2 + SUB_ROWS, :]
        yc = yc + cw[1:2, :] * cu_buf[c0 - 1:c0 - 1 + SUB_ROWS, :]
        yc = yc + cw[2:3, :] * cu
        y_a = b_g * yc * _silu(z_a)

        p0 = POOL_PAD + r0
        v_buf[p0:p0 + SUB_ROWS, :] = v
        pooled = []
        for gi, w in enumerate(POOL_WINDOWS):
            lo = gi * POOL_GC
            vg = v[:, lo:lo + POOL_GC]
            acc = vg
            for k in range(1, w):
                acc = acc + v_buf[p0 - k:p0 - k + SUB_ROWS, lo:lo + POOL_GC]
            if s == 0:
                cnt = jnp.minimum(pos, w).astype(_F32)
                mean = jnp.concatenate([acc[0:head] / cnt, acc[head:] * (1.0 / w)], axis=0)
            else:
                mean = acc * (1.0 / w)
            pooled.append(mean - vg)
        y_b = _pool_project(pooled, pw_ref) * ps_ref[...] * _silu(z_b)

        ycat = jnp.concatenate([y_a, y_b], axis=-1)
        y_ref[0, r0:r0 + SUB_ROWS, :] = _rmsnorm(x + _dot(ycat, wout_ref[...]), fg_ref[...])

    @pl.when(j == pl.num_programs(1) - 1)
    def _():
        nc_ref[0, 0] = cu_buf[CONV_PAD + ts - (CONV_W - 1):CONV_PAD + ts, :]
        np_ref[0, 0] = v_buf[POOL_PAD + ts - (POOL_MAX - 1):POOL_PAD + ts, :]

    cu_buf[0:CONV_PAD, :] = cu_buf[ts:ts + CONV_PAD, :]
    v_buf[0:POOL_PAD, :] = v_buf[ts:ts + POOL_PAD, :]


def _sample_kernel(x_ref, sc_ref, sp_ref, ng_ref, win_ref, cw_ref, pw_ref, ps_ref, wout_ref,
                   fg_ref, y_ref, nc_ref, np_ref):
    n_seq, n_tok, _ = x_ref.shape
    x = jnp.concatenate([x_ref[:, t, :] for t in range(n_tok)], axis=0)
    proj = _dot(_rmsnorm(x, ng_ref[...]), win_ref[...])
    b_g, c_g, u, z_a, v, z_b = _split_proj(proj)

    def tok(a, t):
        return a[t * n_seq:(t + 1) * n_seq, :]

    cu = c_g * u
    cw = cw_ref[...]
    up = [sc_ref[:, k, :] for k in range(CONV_W - 1)] + [tok(cu, t) for t in range(n_tok)]
    yc = []
    for t in range(n_tok):
        acc = cw[0:1, :] * up[t]
        for k in range(1, CONV_W):
            acc = acc + cw[k:k + 1, :] * up[t + k]
        yc.append(acc)
    y_a = b_g * jnp.concatenate(yc, axis=0) * _silu(z_a)
    for k in range(CONV_W - 1):
        nc_ref[:, k, :] = up[n_tok + k]

    hist = POOL_MAX - 1
    vp = [sp_ref[:, k, :] for k in range(hist)] + [tok(v, t) for t in range(n_tok)]
    pooled = []
    for gi, w in enumerate(POOL_WINDOWS):
        lo = gi * POOL_GC
        rows = []
        for t in range(n_tok):
            vg = vp[hist + t][:, lo:lo + POOL_GC]
            acc = vg
            for k in range(1, w):
                acc = acc + vp[hist + t - k][:, lo:lo + POOL_GC]
            rows.append(acc * (1.0 / w) - vg)
        pooled.append(jnp.concatenate(rows, axis=0))
    for k in range(hist):
        np_ref[:, k, :] = vp[n_tok + k]

    y_b = _pool_project(pooled, pw_ref) * ps_ref[...] * _silu(z_b)
    ycat = jnp.concatenate([y_a, y_b], axis=-1)
    y = _rmsnorm(x + _dot(ycat, wout_ref[...]), fg_ref[...])
    for t in range(n_tok):
        y_ref[:, t, :] = tok(y, t)


def _const_spec(shape):
    nd = len(shape)
    return pl.BlockSpec(shape, lambda *_: (0,) * nd, pipeline_mode=pl.Buffered(1))


def _weight_specs():
    return [
        _const_spec((1, D_MODEL)),
        _const_spec((D_MODEL, D_IN)),
        _const_spec((CONV_W, D_CONV)),
        _const_spec((len(POOL_WINDOWS), POOL_GC, POOL_GC)),
        _const_spec((1, D_POOL)),
        _const_spec((D_MODEL, D_MODEL)),
        _const_spec((1, D_MODEL)),
    ]


def _prompt_call(x, weights):
    batch, seq, _ = x.shape
    assert seq % SEQ_TILE == 0 and SEQ_TILE % SUB_ROWS == 0
    n_tiles = seq // SEQ_TILE
    return pl.pallas_call(
        _prompt_kernel,
        grid=(batch, n_tiles),
        in_specs=[pl.BlockSpec((1, SEQ_TILE, D_MODEL), lambda b, j: (b, j, 0))] + _weight_specs(),
        out_specs=[
            pl.BlockSpec((1, SEQ_TILE, D_MODEL), lambda b, j: (b, j, 0)),
            pl.BlockSpec((1, 1, CONV_W - 1, D_CONV), lambda b, j: (0, b, 0, 0)),
            pl.BlockSpec((1, 1, POOL_MAX - 1, D_POOL), lambda b, j: (0, b, 0, 0)),
        ],
        out_shape=[
            jax.ShapeDtypeStruct((batch, seq, D_MODEL), _F32),
            jax.ShapeDtypeStruct((1, batch, CONV_W - 1, D_CONV), _F32),
            jax.ShapeDtypeStruct((1, batch, POOL_MAX - 1, D_POOL), _F32),
        ],
        scratch_shapes=[
            pltpu.VMEM((CONV_PAD + SEQ_TILE, D_CONV), _F32),
            pltpu.VMEM((POOL_PAD + SEQ_TILE, D_POOL), _F32),
        ],
        compiler_params=pltpu.CompilerParams(
            dimension_semantics=("arbitrary", "arbitrary"),
            vmem_limit_bytes=VMEM_LIMIT_BYTES),
        name="prompt_mixer",
    )(x, *weights)


def _sample_call(x, sc, sp, weights):
    n_seq, n_tok, _ = x.shape
    shapes = [(n_seq, n_tok, D_MODEL), (n_seq, CONV_W - 1, D_CONV), (n_seq, POOL_MAX - 1, D_POOL)]
    return pl.pallas_call(
        _sample_kernel,
        grid=(1,),
        in_specs=[_const_spec(sh) for sh in shapes] + _weight_specs(),
        out_specs=[pl.BlockSpec(sh, lambda i: (0, 0, 0)) for sh in shapes],
        out_shape=[jax.ShapeDtypeStruct(sh, _F32) for sh in shapes],
        compiler_params=pltpu.CompilerParams(
            dimension_semantics=("arbitrary",),
            vmem_limit_bytes=VMEM_LIMIT_BYTES),
        name="sample_mixer",
    )(x, sc, sp, *weights)


def kernel(x_prompt, x_sample, state_conv, state_pool, norm_g, w_in, conv_w, pool_w, pool_scale,
           w_out, final_g):
    assert norm_g.shape[0] == 1, "single-layer trunk"
    weights = (norm_g, w_in[0], conv_w[0], pool_w[0], pool_scale, w_out[0],
               final_g.reshape(1, D_MODEL))
    y_prompt, nc_p, np_p = _prompt_call(x_prompt, weights)
    y_sample, nc_s, np_s = _sample_call(x_sample, state_conv[0], state_pool[0], weights)
    return (y_prompt, y_sample, nc_p, np_p, nc_s[None], np_s[None])
```

```python
import jax
import jax.numpy as jnp
from jax import lax
from jax.experimental import pallas as pl
from jax.experimental.pallas import tpu as pltpu

D_MODEL = 1024
D_CONV = 512
D_POOL = 512
POOL_WINDOWS = (2, 4, 8, 16)
POOL_GC = D_POOL // len(POOL_WINDOWS)
POOL_MAX = max(POOL_WINDOWS)
CONV_W = 3
D_IN = 4 * D_CONV + 2 * D_POOL
EPS = 1e-6

SEQ_TILE = 1024
SUB_ROWS = 512
CONV_PAD = 8
POOL_PAD = 16
VMEM_LIMIT_BYTES = 56 * 1024 * 1024

_F32 = jnp.float32

_COL_B, _COL_C, _COL_U, _COL_ZA = 0, D_CONV, 2 * D_CONV, 3 * D_CONV
_COL_V, _COL_ZB = 4 * D_CONV, 4 * D_CONV + D_POOL


def _rmsnorm(x, g):
    ms = jnp.mean(x * x, axis=-1, keepdims=True)
    return (x * lax.rsqrt(ms + EPS)) * g


def _silu(z):
    return z * jax.nn.sigmoid(z)


def _dot(a, b):
    return jnp.dot(a, b, precision=lax.Precision.DEFAULT, preferred_element_type=_F32)


def _split_proj(proj):
    return tuple(proj[:, c:c + D_CONV] for c in (_COL_B, _COL_C, _COL_U, _COL_ZA, _COL_V, _COL_ZB))


def _pool_project(pooled_groups, pw_ref):
    outs = [_dot(p, pw_ref[gi]) for gi, p in enumerate(pooled_groups)]
    return jnp.concatenate(outs, axis=-1)


def _prompt_kernel(x_ref, ng_ref, win_ref, cw_ref, pw_ref, ps_ref, wout_ref, fg_ref,
                   y_ref, nc_ref, np_ref, cu_buf, v_buf):
    j = pl.program_id(1)
    ts = SEQ_TILE
    head = POOL_MAX

    @pl.when(j == 0)
    def _():
        cu_buf[0:CONV_PAD, :] = jnp.zeros((CONV_PAD, D_CONV), _F32)
        v_buf[0:POOL_PAD, :] = jnp.zeros((POOL_PAD, D_POOL), _F32)

    cw = [cw_ref[k] for k in range(CONV_W)]
    pos = j * ts + 1 + lax.broadcasted_iota(jnp.int32, (head, POOL_GC), 0)

    for s in range(ts // SUB_ROWS):
        r0 = s * SUB_ROWS
        x = x_ref[0, r0:r0 + SUB_ROWS, :]
        proj = _dot(_rmsnorm(x, ng_ref[...]), win_ref[...])
        b_g, c_g, u, z_a, v, z_b = _split_proj(proj)

        cu = c_g * u
        c0 = CONV_PAD + r0
        cu_buf[c0:c0 + SUB_ROWS, :] = cu
        yc = cw[0] * cu_buf[c0 - 2:c0 - 2 + SUB_ROWS, :]
        yc = yc + cw[1] * cu_buf[c0 - 1:c0 - 1 + SUB_ROWS, :]
        yc = yc + cw[2] * cu
        y_a = b_g * yc * _silu(z_a)

        p0 = POOL_PAD + r0
        v_buf[p0:p0 + SUB_ROWS, :] = v
        pooled = []
        for gi, w in enumerate(POOL_WINDOWS):
            lo = gi * POOL_GC
            vg = v[:, lo:lo + POOL_GC]
            acc = vg
            for k in range(1, w):
                acc = acc + v_buf[p0 - k:p0 - k + SUB_ROWS, lo:lo + POOL_GC]
            if s == 0:
                cnt = jnp.minimum(pos, w).astype(_F32)
                mean = jnp.concatenate([acc[0:head] / cnt, acc[head:] * (1.0 / w)], axis=0)
            else:
                mean = acc * (1.0 / w)
            pooled.append(mean - vg)
        y_b = _pool_project(pooled, pw_ref) * ps_ref[...] * _silu(z_b)

        ycat = jnp.concatenate([y_a, y_b], axis=-1)
        y_ref[0, r0:r0 + SUB_ROWS, :] = _rmsnorm(x + _dot(ycat, wout_ref[...]), fg_ref[...])

    @pl.when(j == pl.num_programs(1) - 1)
    def _():
        nc_ref[0, 0] = cu_buf[CONV_PAD + ts - (CONV_W - 1):CONV_PAD + ts, :]

    hist = POOL_MAX - 1
    tail0 = POOL_PAD + ts - hist
    b = pl.program_id(0)
    n_seq = np_ref.shape[1]

    @pl.when((j == pl.num_programs(1) - 1) & (b == 0))
    def _():
        for k in range(hist):
            np_ref[k] = jnp.broadcast_to(v_buf[tail0 + k:tail0 + k + 1, :], (n_seq, D_POOL))

    @pl.when((j == pl.num_programs(1) - 1) & (b > 0))
    def _():
        mine = lax.broadcasted_iota(jnp.int32, (n_seq, D_POOL), 0) == b
        for k in range(hist):
            row = jnp.broadcast_to(v_buf[tail0 + k:tail0 + k + 1, :], (n_seq, D_POOL))
            np_ref[k] = jnp.where(mine, row, np_ref[k])

    cu_buf[0:CONV_PAD, :] = cu_buf[ts:ts + CONV_PAD, :]
    v_buf[0:POOL_PAD, :] = v_buf[ts:ts + POOL_PAD, :]


def _sample_kernel(x_ref, sc_ref, sp_ref, ng_ref, win_ref, cw_ref, pw_ref, ps_ref, wout_ref,
                   fg_ref, y_ref, nc_ref, np_ref):
    n_seq, n_tok, _ = x_ref.shape
    x = jnp.concatenate([x_ref[:, t, :] for t in range(n_tok)], axis=0)
    proj = _dot(_rmsnorm(x, ng_ref[...]), win_ref[...])
    b_g, c_g, u, z_a, v, z_b = _split_proj(proj)

    def tok(a, t):
        return a[t * n_seq:(t + 1) * n_seq, :]

    cu = c_g * u
    cw = [cw_ref[k] for k in range(CONV_W)]
    up = [sc_ref[:, k, :] for k in range(CONV_W - 1)] + [tok(cu, t) for t in range(n_tok)]
    yc = []
    for t in range(n_tok):
        acc = cw[0] * up[t]
        for k in range(1, CONV_W):
            acc = acc + cw[k] * up[t + k]
        yc.append(acc)
    y_a = b_g * jnp.concatenate(yc, axis=0) * _silu(z_a)
    for k in range(CONV_W - 1):
        nc_ref[:, k, :] = up[n_tok + k]

    hist = POOL_MAX - 1
    vp = [sp_ref[k] for k in range(hist)] + [tok(v, t) for t in range(n_tok)]
    pooled = []
    for gi, w in enumerate(POOL_WINDOWS):
        lo = gi * POOL_GC
        rows = []
        for t in range(n_tok):
            vg = vp[hist + t][:, lo:lo + POOL_GC]
            acc = vg
            for k in range(1, w):
                acc = acc + vp[hist + t - k][:, lo:lo + POOL_GC]
            rows.append(acc * (1.0 / w) - vg)
        pooled.append(jnp.concatenate(rows, axis=0))
    for k in range(hist):
        np_ref[k] = vp[n_tok + k]

    y_b = _pool_project(pooled, pw_ref) * ps_ref[...] * _silu(z_b)
    ycat = jnp.concatenate([y_a, y_b], axis=-1)
    y = _rmsnorm(x + _dot(ycat, wout_ref[...]), fg_ref[...])
    for t in range(n_tok):
        y_ref[:, t, :] = tok(y, t)


def _const_spec(shape):
    nd = len(shape)
    return pl.BlockSpec(shape, lambda *_: (0,) * nd, pipeline_mode=pl.Buffered(1))


def _weight_specs():
    return [
        _const_spec((1, D_MODEL)),
        _const_spec((D_MODEL, D_IN)),
        _const_spec((CONV_W, 1, D_CONV)),
        _const_spec((len(POOL_WINDOWS), POOL_GC, POOL_GC)),
        _const_spec((1, D_POOL)),
        _const_spec((D_MODEL, D_MODEL)),
        _const_spec((1, D_MODEL)),
    ]


def _prompt_call(x, weights):
    batch, seq, _ = x.shape
    assert seq % SEQ_TILE == 0 and SEQ_TILE % SUB_ROWS == 0
    n_tiles = seq // SEQ_TILE
    return pl.pallas_call(
        _prompt_kernel,
        grid=(batch, n_tiles),
        in_specs=[pl.BlockSpec((1, SEQ_TILE, D_MODEL), lambda b, j: (b, j, 0))] + _weight_specs(),
        out_specs=[
            pl.BlockSpec((1, SEQ_TILE, D_MODEL), lambda b, j: (b, j, 0)),
            pl.BlockSpec((1, 1, CONV_W - 1, D_CONV), lambda b, j: (0, b, 0, 0)),
            pl.BlockSpec((POOL_MAX - 1, batch, D_POOL), lambda b, j: (0, 0, 0)),
        ],
        out_shape=[
            jax.ShapeDtypeStruct((batch, seq, D_MODEL), _F32),
            jax.ShapeDtypeStruct((1, batch, CONV_W - 1, D_CONV), _F32),
            jax.ShapeDtypeStruct((POOL_MAX - 1, batch, D_POOL), _F32),
        ],
        scratch_shapes=[
            pltpu.VMEM((CONV_PAD + SEQ_TILE, D_CONV), _F32),
            pltpu.VMEM((POOL_PAD + SEQ_TILE, D_POOL), _F32),
        ],
        compiler_params=pltpu.CompilerParams(
            dimension_semantics=("arbitrary", "arbitrary"),
            vmem_limit_bytes=VMEM_LIMIT_BYTES),
        name="prompt_mixer",
    )(x, *weights)


def _sample_call(x, sc, sp, weights):
    n_seq, n_tok, _ = x.shape
    shapes = [(n_seq, n_tok, D_MODEL), (n_seq, CONV_W - 1, D_CONV), (POOL_MAX - 1, n_seq, D_POOL)]
    return pl.pallas_call(
        _sample_kernel,
        grid=(1,),
        in_specs=[_const_spec(sh) for sh in shapes] + _weight_specs(),
        out_specs=[pl.BlockSpec(sh, lambda i: (0, 0, 0)) for sh in shapes],
        out_shape=[jax.ShapeDtypeStruct(sh, _F32) for sh in shapes],
        compiler_params=pltpu.CompilerParams(
            dimension_semantics=("arbitrary",),
            vmem_limit_bytes=VMEM_LIMIT_BYTES),
        name="sample_mixer",
    )(x, sc, sp, *weights)


def kernel(x_prompt, x_sample, state_conv, state_pool, norm_g, w_in, conv_w, pool_w, pool_scale,
           w_out, final_g):
    assert norm_g.shape[0] == 1, "single-layer trunk"
    weights = (norm_g, w_in[0], jnp.swapaxes(conv_w, 0, 1), pool_w[0], pool_scale, w_out[0],
               final_g.reshape(1, D_MODEL))
    y_prompt, nc_p, np_p = _prompt_call(x_prompt, weights)
    y_sample, nc_s, np_s = _sample_call(x_sample, state_conv[0], jnp.swapaxes(state_pool[0], 0, 1),
                                        weights)
    return (y_prompt, y_sample, nc_p, jnp.swapaxes(np_p, 0, 1)[None], nc_s[None],
            jnp.swapaxes(np_s, 0, 1)[None])
```

```python
import jax
import jax.numpy as jnp
from jax import lax
from jax.experimental import pallas as pl
from jax.experimental.pallas import tpu as pltpu

D_MODEL = 1024
D_CONV = 512
D_POOL = 512
POOL_WINDOWS = (2, 4, 8, 16)
POOL_GC = D_POOL // len(POOL_WINDOWS)
POOL_MAX = max(POOL_WINDOWS)
CONV_W = 3
D_IN = 4 * D_CONV + 2 * D_POOL
EPS = 1e-6

SEQ_TILE = 1024
SUB_ROWS = 512
CONV_PAD = 8
POOL_PAD = 16
VMEM_LIMIT_BYTES = 56 * 1024 * 1024

_F32 = jnp.float32

_COL_B, _COL_C, _COL_U, _COL_ZA = 0, D_CONV, 2 * D_CONV, 3 * D_CONV
_COL_V, _COL_ZB = 4 * D_CONV, 4 * D_CONV + D_POOL


def _rmsnorm(x, g):
    ms = jnp.mean(x * x, axis=-1, keepdims=True)
    return (x * lax.rsqrt(ms + EPS)) * g


def _silu(z):
    return z * jax.nn.sigmoid(z)


def _dot(a, b):
    return jnp.dot(a, b, precision=lax.Precision.DEFAULT, preferred_element_type=_F32)


def _split_proj(proj):
    return tuple(proj[:, c:c + D_CONV] for c in (_COL_B, _COL_C, _COL_U, _COL_ZA, _COL_V, _COL_ZB))


def _pool_project(pooled_groups, pw_ref):
    outs = [_dot(p, pw_ref[gi]) for gi, p in enumerate(pooled_groups)]
    return jnp.concatenate(outs, axis=-1)


def _window_sums(ext):
    assert all(b == 2 * a for a, b in zip((1,) + POOL_WINDOWS, POOL_WINDOWS)) and POOL_PAD >= POOL_MAX
    sums, width, out = ext, 1, []
    for w in POOL_WINDOWS:
        while width < w:
            sums = sums + pltpu.roll(sums, width, axis=0)
            width *= 2
        out.append(sums[POOL_PAD:, 0:POOL_GC])
        sums = sums[:, POOL_GC:]
    return out


def _prompt_kernel(x_ref, ng_ref, win_ref, cw_ref, pw_ref, ps_ref, wout_ref, fg_ref,
                   y_ref, nc_ref, np_ref, cu_buf, v_buf):
    j = pl.program_id(1)
    ts = SEQ_TILE
    head = POOL_MAX

    @pl.when(j == 0)
    def _():
        cu_buf[0:CONV_PAD, :] = jnp.zeros((CONV_PAD, D_CONV), _F32)
        v_buf[0:POOL_PAD, :] = jnp.zeros((POOL_PAD, D_POOL), _F32)

    cw = [cw_ref[k] for k in range(CONV_W)]
    pos = j * ts + 1 + lax.broadcasted_iota(jnp.int32, (head, POOL_GC), 0)

    n_sub = ts // SUB_ROWS
    xs = [x_ref[0, s * SUB_ROWS:(s + 1) * SUB_ROWS, :] for s in range(n_sub)]
    hs = [_rmsnorm(x, ng_ref[...]) for x in xs]
    vs = [_dot(h, win_ref[:, _COL_V:_COL_V + D_POOL]) for h in hs]
    zbs = [_dot(h, win_ref[:, _COL_ZB:_COL_ZB + D_POOL]) for h in hs]
    projs = [_dot(h, win_ref[:, _COL_B:_COL_ZA + D_CONV]) for h in hs]

    for s in range(n_sub):
        r0 = s * SUB_ROWS
        x, v, z_b = xs[s], vs[s], zbs[s]
        b_g, c_g, u, z_a = (projs[s][:, c:c + D_CONV] for c in (_COL_B, _COL_C, _COL_U, _COL_ZA))

        p0 = POOL_PAD + r0
        v_buf[p0:p0 + SUB_ROWS, :] = v
        wsums = _window_sums(v_buf[p0 - POOL_PAD:p0 + SUB_ROWS, :])
        pooled = []
        for gi, w in enumerate(POOL_WINDOWS):
            acc = wsums[gi]
            if s == 0:
                cnt = jnp.minimum(pos, w).astype(_F32)
                mean = jnp.concatenate([acc[0:head] / cnt, acc[head:] * (1.0 / w)], axis=0)
            else:
                mean = acc * (1.0 / w)
            pooled.append(mean - v[:, gi * POOL_GC:(gi + 1) * POOL_GC])

        cu = c_g * u
        c0 = CONV_PAD + r0
        cu_buf[c0:c0 + SUB_ROWS, :] = cu
        yc = cw[0] * cu_buf[c0 - 2:c0 - 2 + SUB_ROWS, :]
        yc = yc + cw[1] * cu_buf[c0 - 1:c0 - 1 + SUB_ROWS, :]
        yc = yc + cw[2] * cu
        y_a = b_g * yc * _silu(z_a)

        y_b = _pool_project(pooled, pw_ref) * ps_ref[...] * _silu(z_b)

        ycat = jnp.concatenate([y_a, y_b], axis=-1)
        y_ref[0, r0:r0 + SUB_ROWS, :] = _rmsnorm(x + _dot(ycat, wout_ref[...]), fg_ref[...])

    @pl.when(j == pl.num_programs(1) - 1)
    def _():
        nc_ref[0, 0] = cu_buf[CONV_PAD + ts - (CONV_W - 1):CONV_PAD + ts, :]

    hist = POOL_MAX - 1
    tail0 = POOL_PAD + ts - hist
    b = pl.program_id(0)
    n_seq = np_ref.shape[1]

    @pl.when((j == pl.num_programs(1) - 1) & (b == 0))
    def _():
        for k in range(hist):
            np_ref[k] = jnp.broadcast_to(v_buf[tail0 + k:tail0 + k + 1, :], (n_seq, D_POOL))

    @pl.when((j == pl.num_programs(1) - 1) & (b > 0))
    def _():
        mine = lax.broadcasted_iota(jnp.int32, (n_seq, D_POOL), 0) == b
        for k in range(hist):
            row = jnp.broadcast_to(v_buf[tail0 + k:tail0 + k + 1, :], (n_seq, D_POOL))
            np_ref[k] = jnp.where(mine, row, np_ref[k])

    cu_buf[0:CONV_PAD, :] = cu_buf[ts:ts + CONV_PAD, :]
    v_buf[0:POOL_PAD, :] = v_buf[ts:ts + POOL_PAD, :]


def _sample_kernel(x_ref, sc_ref, sp_ref, ng_ref, win_ref, cw_ref, pw_ref, ps_ref, wout_ref,
                   fg_ref, y_ref, nc_ref, np_ref):
    n_seq, n_tok, _ = x_ref.shape
    x = jnp.concatenate([x_ref[:, t, :] for t in range(n_tok)], axis=0)
    proj = _dot(_rmsnorm(x, ng_ref[...]), win_ref[...])
    b_g, c_g, u, z_a, v, z_b = _split_proj(proj)

    def tok(a, t):
        return a[t * n_seq:(t + 1) * n_seq, :]

    cu = c_g * u
    cw = [cw_ref[k] for k in range(CONV_W)]
    up = [sc_ref[:, k, :] for k in range(CONV_W - 1)] + [tok(cu, t) for t in range(n_tok)]
    yc = []
    for t in range(n_tok):
        acc = cw[0] * up[t]
        for k in range(1, CONV_W):
            acc = acc + cw[k] * up[t + k]
        yc.append(acc)
    y_a = b_g * jnp.concatenate(yc, axis=0) * _silu(z_a)
    for k in range(CONV_W - 1):
        nc_ref[:, k, :] = up[n_tok + k]

    hist = POOL_MAX - 1
    vp = [sp_ref[k] for k in range(hist)] + [tok(v, t) for t in range(n_tok)]
    pooled = []
    for gi, w in enumerate(POOL_WINDOWS):
        lo = gi * POOL_GC
        rows = []
        for t in range(n_tok):
            vg = vp[hist + t][:, lo:lo + POOL_GC]
            acc = vg
            for k in range(1, w):
                acc = acc + vp[hist + t - k][:, lo:lo + POOL_GC]
            rows.append(acc * (1.0 / w) - vg)
        pooled.append(jnp.concatenate(rows, axis=0))
    for k in range(hist):
        np_ref[k] = vp[n_tok + k]

    y_b = _pool_project(pooled, pw_ref) * ps_ref[...] * _silu(z_b)
    ycat = jnp.concatenate([y_a, y_b], axis=-1)
    y = _rmsnorm(x + _dot(ycat, wout_ref[...]), fg_ref[...])
    for t in range(n_tok):
        y_ref[:, t, :] = tok(y, t)


def _const_spec(shape):
    nd = len(shape)
    return pl.BlockSpec(shape, lambda *_: (0,) * nd, pipeline_mode=pl.Buffered(1))


def _weight_specs():
    return [
        _const_spec((1, D_MODEL)),
        _const_spec((D_MODEL, D_IN)),
        _const_spec((CONV_W, 1, D_CONV)),
        _const_spec((len(POOL_WINDOWS), POOL_GC, POOL_GC)),
        _const_spec((1, D_POOL)),
        _const_spec((D_MODEL, D_MODEL)),
        _const_spec((1, D_MODEL)),
    ]


def _prompt_call(x, weights):
    batch, seq, _ = x.shape
    assert seq % SEQ_TILE == 0 and SEQ_TILE % SUB_ROWS == 0
    n_tiles = seq // SEQ_TILE
    return pl.pallas_call(
        _prompt_kernel,
        grid=(batch, n_tiles),
        in_specs=[pl.BlockSpec((1, SEQ_TILE, D_MODEL), lambda b, j: (b, j, 0))] + _weight_specs(),
        out_specs=[
            pl.BlockSpec((1, SEQ_TILE, D_MODEL), lambda b, j: (b, j, 0)),
            pl.BlockSpec((1, 1, CONV_W - 1, D_CONV), lambda b, j: (0, b, 0, 0)),
            pl.BlockSpec((POOL_MAX - 1, batch, D_POOL), lambda b, j: (0, 0, 0)),
        ],
        out_shape=[
            jax.ShapeDtypeStruct((batch, seq, D_MODEL), _F32),
            jax.ShapeDtypeStruct((1, batch, CONV_W - 1, D_CONV), _F32),
            jax.ShapeDtypeStruct((POOL_MAX - 1, batch, D_POOL), _F32),
        ],
        scratch_shapes=[
            pltpu.VMEM((CONV_PAD + SEQ_TILE, D_CONV), _F32),
            pltpu.VMEM((POOL_PAD + SEQ_TILE, D_POOL), _F32),
        ],
        compiler_params=pltpu.CompilerParams(
            dimension_semantics=("arbitrary", "arbitrary"),
            vmem_limit_bytes=VMEM_LIMIT_BYTES),
        name="prompt_mixer",
    )(x, *weights)


def _sample_call(x, sc, sp, weights):
    n_seq, n_tok, _ = x.shape
    shapes = [(n_seq, n_tok, D_MODEL), (n_seq, CONV_W - 1, D_CONV), (POOL_MAX - 1, n_seq, D_POOL)]
    return pl.pallas_call(
        _sample_kernel,
        grid=(1,),
        in_specs=[_const_spec(sh) for sh in shapes] + _weight_specs(),
        out_specs=[pl.BlockSpec(sh, lambda i: (0, 0, 0)) for sh in shapes],
        out_shape=[jax.ShapeDtypeStruct(sh, _F32) for sh in shapes],
        compiler_params=pltpu.CompilerParams(
            dimension_semantics=("arbitrary",),
            vmem_limit_bytes=VMEM_LIMIT_BYTES),
        name="sample_mixer",
    )(x, sc, sp, *weights)


def kernel(x_prompt, x_sample, state_conv, state_pool, norm_g, w_in, conv_w, pool_w, pool_scale,
           w_out, final_g):
    assert norm_g.shape[0] == 1, "single-layer trunk"
    weights = (norm_g, w_in[0], jnp.swapaxes(conv_w, 0, 1), pool_w[0], pool_scale, w_out[0],
               final_g.reshape(1, D_MODEL))
    y_prompt, nc_p, np_p = _prompt_call(x_prompt, weights)
    y_sample, nc_s, np_s = _sample_call(x_sample, state_conv[0], jnp.swapaxes(state_pool[0], 0, 1),
                                        weights)
    return (y_prompt, y_sample, nc_p, jnp.swapaxes(np_p, 0, 1)[None], nc_s[None],
            jnp.swapaxes(np_s, 0, 1)[None])
```

```python
import jax
import jax.numpy as jnp
from jax import lax
from jax.experimental import pallas as pl
from jax.experimental.pallas import tpu as pltpu

D_MODEL = 1024
D_CONV = 512
D_POOL = 512
POOL_WINDOWS = (2, 4, 8, 16)
POOL_GC = D_POOL // len(POOL_WINDOWS)
POOL_MAX = max(POOL_WINDOWS)
CONV_W = 3
D_IN = 4 * D_CONV + 2 * D_POOL
EPS = 1e-6

SEQ_TILE = 1024
SUB_ROWS = 512
CONV_PAD = 8
POOL_PAD = 16
VMEM_LIMIT_BYTES = 56 * 1024 * 1024

_F32 = jnp.float32

_COL_B, _COL_C, _COL_U, _COL_ZA = 0, D_CONV, 2 * D_CONV, 3 * D_CONV
_COL_V, _COL_ZB = 4 * D_CONV, 4 * D_CONV + D_POOL


def _rmsnorm(x, g):
    ms = jnp.mean(x * x, axis=-1, keepdims=True)
    return (x * lax.rsqrt(ms + EPS)) * g


def _silu(z):
    return z * jax.nn.sigmoid(z)


def _dot(a, b):
    return jnp.dot(a, b, precision=lax.Precision.DEFAULT, preferred_element_type=_F32)


def _split_proj(proj):
    return tuple(proj[:, c:c + D_CONV] for c in (_COL_B, _COL_C, _COL_U, _COL_ZA, _COL_V, _COL_ZB))


def _pool_project(pooled_groups, pw_ref):
    outs = [_dot(p, pw_ref[gi]) for gi, p in enumerate(pooled_groups)]
    return jnp.concatenate(outs, axis=-1)


def _window_sums(ext):
    assert all(b == 2 * a for a, b in zip((1,) + POOL_WINDOWS, POOL_WINDOWS)) and POOL_PAD >= POOL_MAX
    sums, width, out = ext, 1, []
    for w in POOL_WINDOWS:
        while width < w:
            sums = sums + pltpu.roll(sums, width, axis=0)
            width *= 2
        out.append(sums[POOL_PAD:, 0:POOL_GC])
        sums = sums[:, POOL_GC:]
    return out


def _prompt_kernel(x_ref, ng_ref, win_ref, cw_ref, pw_ref, ps_ref, wout_ref, fg_ref,
                   y_ref, nc_ref, np_ref, cu_buf, v_buf):
    j = pl.program_id(1)
    ts = SEQ_TILE
    head = POOL_MAX

    @pl.when(j == 0)
    def _():
        cu_buf[0:CONV_PAD, :] = jnp.zeros((CONV_PAD, D_CONV), _F32)
        v_buf[0:POOL_PAD, :] = jnp.zeros((POOL_PAD, D_POOL), _F32)

    cw = [cw_ref[k] for k in range(CONV_W)]
    pos = j * ts + 1 + lax.broadcasted_iota(jnp.int32, (head, POOL_GC), 0)

    n_sub = ts // SUB_ROWS
    xs = [x_ref[0, s * SUB_ROWS:(s + 1) * SUB_ROWS, :] for s in range(n_sub)]
    hs = [_rmsnorm(x, ng_ref[...]) for x in xs]
    vs = [_dot(h, win_ref[:, _COL_V:_COL_V + D_POOL]) for h in hs]
    zbs = [_dot(h, win_ref[:, _COL_ZB:_COL_ZB + D_POOL]) for h in hs]
    projs = [_dot(h, win_ref[:, _COL_B:_COL_ZA + D_CONV]) for h in hs]

    pooled = []
    for s, v in enumerate(vs):
        p0 = POOL_PAD + s * SUB_ROWS
        v_buf[p0:p0 + SUB_ROWS, :] = v
        wsums = _window_sums(v_buf[p0 - POOL_PAD:p0 + SUB_ROWS, :])
        groups = []
        for gi, w in enumerate(POOL_WINDOWS):
            acc = wsums[gi]
            if s == 0:
                cnt = jnp.minimum(pos, w).astype(_F32)
                mean = jnp.concatenate([acc[0:head] / cnt, acc[head:] * (1.0 / w)], axis=0)
            else:
                mean = acc * (1.0 / w)
            groups.append(mean - v[:, gi * POOL_GC:(gi + 1) * POOL_GC])
        pooled.append(groups)

    y_as = []
    for s, proj in enumerate(projs):
        b_g, c_g, u, z_a = (proj[:, c:c + D_CONV] for c in (_COL_B, _COL_C, _COL_U, _COL_ZA))
        cu = c_g * u
        c0 = CONV_PAD + s * SUB_ROWS
        cu_buf[c0:c0 + SUB_ROWS, :] = cu
        yc = cw[0] * cu_buf[c0 - 2:c0 - 2 + SUB_ROWS, :]
        yc = yc + cw[1] * cu_buf[c0 - 1:c0 - 1 + SUB_ROWS, :]
        yc = yc + cw[2] * cu
        y_as.append(b_g * yc * _silu(z_a))

    y_bs = [_pool_project(p, pw_ref) * ps_ref[...] * _silu(z_b) for p, z_b in zip(pooled, zbs)]
    mixes = [_dot(jnp.concatenate([y_a, y_b], axis=-1), wout_ref[...]) for y_a, y_b in zip(y_as, y_bs)]
    for s, (x, mix) in enumerate(zip(xs, mixes)):
        y_ref[0, s * SUB_ROWS:(s + 1) * SUB_ROWS, :] = _rmsnorm(x + mix, fg_ref[...])

    @pl.when(j == pl.num_programs(1) - 1)
    def _():
        nc_ref[0, 0] = cu_buf[CONV_PAD + ts - (CONV_W - 1):CONV_PAD + ts, :]

    hist = POOL_MAX - 1
    tail0 = POOL_PAD + ts - hist
    b = pl.program_id(0)
    n_seq = np_ref.shape[1]

    @pl.when((j == pl.num_programs(1) - 1) & (b == 0))
    def _():
        for k in range(hist):
            np_ref[k] = jnp.broadcast_to(v_buf[tail0 + k:tail0 + k + 1, :], (n_seq, D_POOL))

    @pl.when((j == pl.num_programs(1) - 1) & (b > 0))
    def _():
        mine = lax.broadcasted_iota(jnp.int32, (n_seq, D_POOL), 0) == b
        for k in range(hist):
            row = jnp.broadcast_to(v_buf[tail0 + k:tail0 + k + 1, :], (n_seq, D_POOL))
            np_ref[k] = jnp.where(mine, row, np_ref[k])

    cu_buf[0:CONV_PAD, :] = cu_buf[ts:ts + CONV_PAD, :]
    v_buf[0:POOL_PAD, :] = v_buf[ts:ts + POOL_PAD, :]


def _sample_kernel(x_ref, sc_ref, sp_ref, ng_ref, win_ref, cw_ref, pw_ref, ps_ref, wout_ref,
                   fg_ref, y_ref, nc_ref, np_ref):
    n_seq, n_tok, _ = x_ref.shape
    x = jnp.concatenate([x_ref[:, t, :] for t in range(n_tok)], axis=0)
    proj = _dot(_rmsnorm(x, ng_ref[...]), win_ref[...])
    b_g, c_g, u, z_a, v, z_b = _split_proj(proj)

    def tok(a, t):
        return a[t * n_seq:(t + 1) * n_seq, :]

    cu = c_g * u
    cw = [cw_ref[k] for k in range(CONV_W)]
    up = [sc_ref[:, k, :] for k in range(CONV_W - 1)] + [tok(cu, t) for t in range(n_tok)]
    yc = []
    for t in range(n_tok):
        acc = cw[0] * up[t]
        for k in range(1, CONV_W):
            acc = acc + cw[k] * up[t + k]
        yc.append(acc)
    y_a = b_g * jnp.concatenate(yc, axis=0) * _silu(z_a)
    for k in range(CONV_W - 1):
        nc_ref[:, k, :] = up[n_tok + k]

    hist = POOL_MAX - 1
    vp = [sp_ref[k] for k in range(hist)] + [tok(v, t) for t in range(n_tok)]
    pooled = []
    for gi, w in enumerate(POOL_WINDOWS):
        lo = gi * POOL_GC
        rows = []
        for t in range(n_tok):
            vg = vp[hist + t][:, lo:lo + POOL_GC]
            acc = vg
            for k in range(1, w):
                acc = acc + vp[hist + t - k][:, lo:lo + POOL_GC]
            rows.append(acc * (1.0 / w) - vg)
        pooled.append(jnp.concatenate(rows, axis=0))
    for k in range(hist):
        np_ref[k] = vp[n_tok + k]

    y_b = _pool_project(pooled, pw_ref) * ps_ref[...] * _silu(z_b)
    ycat = jnp.concatenate([y_a, y_b], axis=-1)
    y = _rmsnorm(x + _dot(ycat, wout_ref[...]), fg_ref[...])
    for t in range(n_tok):
        y_ref[:, t, :] = tok(y, t)


def _const_spec(shape):
    nd = len(shape)
    return pl.BlockSpec(shape, lambda *_: (0,) * nd, pipeline_mode=pl.Buffered(1))


def _weight_specs():
    return [
        _const_spec((1, D_MODEL)),
        _const_spec((D_MODEL, D_IN)),
        _const_spec((CONV_W, 1, D_CONV)),
        _const_spec((len(POOL_WINDOWS), POOL_GC, POOL_GC)),
        _const_spec((1, D_POOL)),
        _const_spec((D_MODEL, D_MODEL)),
        _const_spec((1, D_MODEL)),
    ]


def _prompt_call(x, weights):
    batch, seq, _ = x.shape
    assert seq % SEQ_TILE == 0 and SEQ_TILE % SUB_ROWS == 0
    n_tiles = seq // SEQ_TILE
    return pl.pallas_call(
        _prompt_kernel,
        grid=(batch, n_tiles),
        in_specs=[pl.BlockSpec((1, SEQ_TILE, D_MODEL), lambda b, j: (b, j, 0))] + _weight_specs(),
        out_specs=[
            pl.BlockSpec((1, SEQ_TILE, D_MODEL), lambda b, j: (b, j, 0)),
            pl.BlockSpec((1, 1, CONV_W - 1, D_CONV), lambda b, j: (0, b, 0, 0)),
            pl.BlockSpec((POOL_MAX - 1, batch, D_POOL), lambda b, j: (0, 0, 0)),
        ],
        out_shape=[
            jax.ShapeDtypeStruct((batch, seq, D_MODEL), _F32),
            jax.ShapeDtypeStruct((1, batch, CONV_W - 1, D_CONV), _F32),
            jax.ShapeDtypeStruct((POOL_MAX - 1, batch, D_POOL), _F32),
        ],
        scratch_shapes=[
            pltpu.VMEM((CONV_PAD + SEQ_TILE, D_CONV), _F32),
            pltpu.VMEM((POOL_PAD + SEQ_TILE, D_POOL), _F32),
        ],
        compiler_params=pltpu.CompilerParams(
            dimension_semantics=("arbitrary", "arbitrary"),
            vmem_limit_bytes=VMEM_LIMIT_BYTES),
        name="prompt_mixer",
    )(x, *weights)


def _sample_call(x, sc, sp, weights):
    n_seq, n_tok, _ = x.shape
    shapes = [(n_seq, n_tok, D_MODEL), (n_seq, CONV_W - 1, D_CONV), (POOL_MAX - 1, n_seq, D_POOL)]
    return pl.pallas_call(
        _sample_kernel,
        grid=(1,),
        in_specs=[_const_spec(sh) for sh in shapes] + _weight_specs(),
        out_specs=[pl.BlockSpec(sh, lambda i: (0, 0, 0)) for sh in shapes],
        out_shape=[jax.ShapeDtypeStruct(sh, _F32) for sh in shapes],
        compiler_params=pltpu.CompilerParams(
            dimension_semantics=("arbitrary",),
            vmem_limit_bytes=VMEM_LIMIT_BYTES),
        name="sample_mixer",
    )(x, sc, sp, *weights)


def kernel(x_prompt, x_sample, state_conv, state_pool, norm_g, w_in, conv_w, pool_w, pool_scale,
           w_out, final_g):
    assert norm_g.shape[0] == 1, "single-layer trunk"
    weights = (norm_g, w_in[0], jnp.swapaxes(conv_w, 0, 1), pool_w[0], pool_scale, w_out[0],
               final_g.reshape(1, D_MODEL))
    y_prompt, nc_p, np_p = _prompt_call(x_prompt, weights)
    y_sample, nc_s, np_s = _sample_call(x_sample, state_conv[0], jnp.swapaxes(state_pool[0], 0, 1),
                                        weights)
    return (y_prompt, y_sample, nc_p, jnp.swapaxes(np_p, 0, 1)[None], nc_s[None],
            jnp.swapaxes(np_s, 0, 1)[None])
```

```python
import jax
import jax.numpy as jnp
from jax import lax
from jax.experimental import pallas as pl
from jax.experimental.pallas import tpu as pltpu

D_MODEL = 1024
D_CONV = 512
D_POOL = 512
POOL_WINDOWS = (2, 4, 8, 16)
POOL_GC = D_POOL // len(POOL_WINDOWS)
POOL_MAX = max(POOL_WINDOWS)
CONV_W = 3
D_IN = 4 * D_CONV + 2 * D_POOL
EPS = 1e-6

SEQ_TILE = 1024
SUB_ROWS = 512
CONV_PAD = 8
POOL_PAD = 16
VMEM_LIMIT_BYTES = 56 * 1024 * 1024

_F32 = jnp.float32

_COL_B, _COL_C, _COL_U, _COL_ZA = 0, D_CONV, 2 * D_CONV, 3 * D_CONV
_COL_V, _COL_ZB = 4 * D_CONV, 4 * D_CONV + D_POOL


def _rmsnorm(x, g):
    ms = jnp.mean(x * x, axis=-1, keepdims=True)
    return (x * lax.rsqrt(ms + EPS)) * g


def _silu(z):
    return z * jax.nn.sigmoid(z)


def _dot(a, b):
    return jnp.dot(a, b, precision=lax.Precision.DEFAULT, preferred_element_type=_F32)


def _split_proj(proj):
    return tuple(proj[:, c:c + D_CONV] for c in (_COL_B, _COL_C, _COL_U, _COL_ZA, _COL_V, _COL_ZB))


def _pool_project(pooled_groups, pw_ref):
    outs = [_dot(p, pw_ref[gi]) for gi, p in enumerate(pooled_groups)]
    return jnp.concatenate(outs, axis=-1)


def _window_sums(ext):
    assert all(b == 2 * a for a, b in zip((1,) + POOL_WINDOWS, POOL_WINDOWS)) and POOL_PAD >= POOL_MAX
    sums, width, out = ext, 1, []
    for w in POOL_WINDOWS:
        while width < w:
            sums = sums + pltpu.roll(sums, width, axis=0)
            width *= 2
        out.append(sums[POOL_PAD:, 0:POOL_GC])
        sums = sums[:, POOL_GC:]
    return out


def _prompt_kernel(x_ref, ng_ref, win_ref, cw_ref, pw_ref, ps_ref, wout_ref, fg_ref,
                   y_ref, nc_ref, np_ref, cu_buf, v_buf):
    j = pl.program_id(1)
    ts = SEQ_TILE
    head = POOL_MAX

    @pl.when(j == 0)
    def _():
        cu_buf[0:CONV_PAD, :] = jnp.zeros((CONV_PAD, D_CONV), _F32)
        v_buf[0:POOL_PAD, :] = jnp.zeros((POOL_PAD, D_POOL), _F32)

    cw = [cw_ref[k] for k in range(CONV_W)]
    pos = j * ts + 1 + lax.broadcasted_iota(jnp.int32, (head, POOL_GC), 0)

    n_sub = ts // SUB_ROWS
    xs = [x_ref[0, s * SUB_ROWS:(s + 1) * SUB_ROWS, :] for s in range(n_sub)]
    hs = [_rmsnorm(x, ng_ref[...]) for x in xs]
    vs = [_dot(h, win_ref[:, _COL_V:_COL_V + D_POOL]) for h in hs]
    zbs = [_dot(h, win_ref[:, _COL_ZB:_COL_ZB + D_POOL]) for h in hs]
    projs = [_dot(h, win_ref[:, _COL_B:_COL_ZA + D_CONV]) for h in hs]

    pooled = []
    for s, v in enumerate(vs):
        p0 = POOL_PAD + s * SUB_ROWS
        v_buf[p0:p0 + SUB_ROWS, :] = v
        wsums = _window_sums(v_buf[p0 - POOL_PAD:p0 + SUB_ROWS, :])
        groups = []
        for gi, w in enumerate(POOL_WINDOWS):
            acc = wsums[gi]
            if s == 0:
                cnt = jnp.minimum(pos, w).astype(_F32)
                mean = jnp.concatenate([acc[0:head] / cnt, acc[head:] * (1.0 / w)], axis=0)
            else:
                mean = acc * (1.0 / w)
            groups.append(mean - v[:, gi * POOL_GC:(gi + 1) * POOL_GC])
        pooled.append(groups)

    y_as = []
    for s, proj in enumerate(projs):
        b_g, c_g, u, z_a = (proj[:, c:c + D_CONV] for c in (_COL_B, _COL_C, _COL_U, _COL_ZA))
        cu = c_g * u
        c0 = CONV_PAD + s * SUB_ROWS
        cu_buf[c0:c0 + SUB_ROWS, :] = cu
        yc = cw[0] * cu_buf[c0 - 2:c0 - 2 + SUB_ROWS, :]
        yc = yc + cw[1] * cu_buf[c0 - 1:c0 - 1 + SUB_ROWS, :]
        yc = yc + cw[2] * cu
        y_as.append(b_g * yc * _silu(z_a))

    y_bs = [_pool_project(p, pw_ref) * ps_ref[...] * _silu(z_b) for p, z_b in zip(pooled, zbs)]
    mixes = [_dot(jnp.concatenate([y_a, y_b], axis=-1), wout_ref[...]) for y_a, y_b in zip(y_as, y_bs)]
    for s, (x, mix) in enumerate(zip(xs, mixes)):
        y_ref[0, s * SUB_ROWS:(s + 1) * SUB_ROWS, :] = _rmsnorm(x + mix, fg_ref[...])

    @pl.when(j == pl.num_programs(1) - 1)
    def _():
        nc_ref[0, 0] = cu_buf[CONV_PAD + ts - (CONV_W - 1):CONV_PAD + ts, :]

    hist = POOL_MAX - 1
    tail0 = POOL_PAD + ts - hist
    b = pl.program_id(0)
    n_seq = np_ref.shape[1]

    @pl.when((j == pl.num_programs(1) - 1) & (b == 0))
    def _():
        for k in range(hist):
            np_ref[k] = jnp.broadcast_to(v_buf[tail0 + k:tail0 + k + 1, :], (n_seq, D_POOL))

    @pl.when((j == pl.num_programs(1) - 1) & (b > 0))
    def _():
        mine = lax.broadcasted_iota(jnp.int32, (n_seq, D_POOL), 0) == b
        for k in range(hist):
            row = jnp.broadcast_to(v_buf[tail0 + k:tail0 + k + 1, :], (n_seq, D_POOL))
            np_ref[k] = jnp.where(mine, row, np_ref[k])

    cu_buf[0:CONV_PAD, :] = cu_buf[ts:ts + CONV_PAD, :]
    v_buf[0:POOL_PAD, :] = v_buf[ts:ts + POOL_PAD, :]


_SAMPLE_PART_COLS = (_COL_V, _COL_ZB, _COL_B, _COL_C, _COL_U, _COL_ZA)


def _sample_kernel(x_hbm, sc_hbm, sp_hbm, win_hbm, wout_hbm, ng_ref, cw_ref, pw_ref, ps_ref, fg_ref,
                   y_hbm, nc_hbm, np_hbm,
                   x_buf, sc_buf, sp_buf, win_bufs, wout_buf, y_buf, nc_buf, np_buf, in_sems, out_sems):
    n_seq, n_tok, _ = x_buf.shape
    hist = POOL_MAX - 1

    x_cp = pltpu.make_async_copy(x_hbm, x_buf, in_sems.at[0])
    sp_cp = pltpu.make_async_copy(sp_hbm, sp_buf, in_sems.at[1])
    sc_cp = pltpu.make_async_copy(sc_hbm, sc_buf, in_sems.at[2])
    wout_cp = pltpu.make_async_copy(wout_hbm, wout_buf, in_sems.at[3])
    part_cps = [pltpu.make_async_copy(win_hbm.at[:, pl.ds(col, D_CONV)], win_bufs.at[i],
                                      in_sems.at[4 + i])
                for i, col in enumerate(_SAMPLE_PART_COLS)]
    for cp in [x_cp, part_cps[0], sp_cp] + part_cps[1:5] + [sc_cp, part_cps[5], wout_cp]:
        cp.start()

    def tok(a, t):
        return a[t * n_seq:(t + 1) * n_seq, :]

    def part(i, h):
        part_cps[i].wait()
        return _dot(h, win_bufs[i])

    x_cp.wait()
    x = jnp.concatenate([x_buf[:, t, :] for t in range(n_tok)], axis=0)
    h = _rmsnorm(x, ng_ref[...])

    v = part(0, h)
    sp_cp.wait()
    vp = [sp_buf[k] for k in range(hist)] + [tok(v, t) for t in range(n_tok)]
    for k in range(hist):
        np_buf[k] = vp[n_tok + k]
    np_out = pltpu.make_async_copy(np_buf, np_hbm, out_sems.at[0])
    np_out.start()
    pooled = []
    for gi, w in enumerate(POOL_WINDOWS):
        lo = gi * POOL_GC
        rows = []
        for t in range(n_tok):
            vg = vp[hist + t][:, lo:lo + POOL_GC]
            acc = vg
            for k in range(1, w):
                acc = acc + vp[hist + t - k][:, lo:lo + POOL_GC]
            rows.append(acc * (1.0 / w) - vg)
        pooled.append(jnp.concatenate(rows, axis=0))
    z_b = part(1, h)
    y_b = _pool_project(pooled, pw_ref) * ps_ref[...] * _silu(z_b)

    b_g = part(2, h)
    cu = part(3, h) * part(4, h)
    sc_cp.wait()
    cw = [cw_ref[k] for k in range(CONV_W)]
    up = [sc_buf[:, k, :] for k in range(CONV_W - 1)] + [tok(cu, t) for t in range(n_tok)]
    for k in range(CONV_W - 1):
        nc_buf[:, k, :] = up[n_tok + k]
    nc_out = pltpu.make_async_copy(nc_buf, nc_hbm, out_sems.at[1])
    nc_out.start()
    yc = []
    for t in range(n_tok):
        acc = cw[0] * up[t]
        for k in range(1, CONV_W):
            acc = acc + cw[k] * up[t + k]
        yc.append(acc)
    y_a = b_g * jnp.concatenate(yc, axis=0) * _silu(part(5, h))

    wout_cp.wait()
    ycat = jnp.concatenate([y_a, y_b], axis=-1)
    y = _rmsnorm(x + _dot(ycat, wout_buf[...]), fg_ref[...])
    for t in range(n_tok):
        y_buf[:, t, :] = tok(y, t)
    y_out = pltpu.make_async_copy(y_buf, y_hbm, out_sems.at[2])
    y_out.start()
    np_out.wait()
    nc_out.wait()
    y_out.wait()


def _const_spec(shape):
    nd = len(shape)
    return pl.BlockSpec(shape, lambda *_: (0,) * nd, pipeline_mode=pl.Buffered(1))


def _weight_specs():
    return [
        _const_spec((1, D_MODEL)),
        _const_spec((D_MODEL, D_IN)),
        _const_spec((CONV_W, 1, D_CONV)),
        _const_spec((len(POOL_WINDOWS), POOL_GC, POOL_GC)),
        _const_spec((1, D_POOL)),
        _const_spec((D_MODEL, D_MODEL)),
        _const_spec((1, D_MODEL)),
    ]


def _prompt_call(x, weights):
    batch, seq, _ = x.shape
    assert seq % SEQ_TILE == 0 and SEQ_TILE % SUB_ROWS == 0
    n_tiles = seq // SEQ_TILE
    return pl.pallas_call(
        _prompt_kernel,
        grid=(batch, n_tiles),
        in_specs=[pl.BlockSpec((1, SEQ_TILE, D_MODEL), lambda b, j: (b, j, 0))] + _weight_specs(),
        out_specs=[
            pl.BlockSpec((1, SEQ_TILE, D_MODEL), lambda b, j: (b, j, 0)),
            pl.BlockSpec((1, 1, CONV_W - 1, D_CONV), lambda b, j: (0, b, 0, 0)),
            pl.BlockSpec((POOL_MAX - 1, batch, D_POOL), lambda b, j: (0, 0, 0)),
        ],
        out_shape=[
            jax.ShapeDtypeStruct((batch, seq, D_MODEL), _F32),
            jax.ShapeDtypeStruct((1, batch, CONV_W - 1, D_CONV), _F32),
            jax.ShapeDtypeStruct((POOL_MAX - 1, batch, D_POOL), _F32),
        ],
        scratch_shapes=[
            pltpu.VMEM((CONV_PAD + SEQ_TILE, D_CONV), _F32),
            pltpu.VMEM((POOL_PAD + SEQ_TILE, D_POOL), _F32),
        ],
        compiler_params=pltpu.CompilerParams(
            dimension_semantics=("arbitrary", "arbitrary"),
            vmem_limit_bytes=VMEM_LIMIT_BYTES),
        name="prompt_mixer",
    )(x, *weights)


def _sample_call(x, sc, sp, weights):
    n_seq, n_tok, _ = x.shape
    shapes = [(n_seq, n_tok, D_MODEL), (n_seq, CONV_W - 1, D_CONV), (POOL_MAX - 1, n_seq, D_POOL)]
    ng, w_in, cw, pw, ps, w_out, fg = weights
    small = [ng, cw, pw, ps, fg]
    hbm = pl.BlockSpec(memory_space=pl.ANY)
    n_parts = len(_SAMPLE_PART_COLS)
    return pl.pallas_call(
        _sample_kernel,
        grid=(1,),
        in_specs=[hbm] * 5 + [_const_spec(a.shape) for a in small],
        out_specs=[hbm] * 3,
        out_shape=[jax.ShapeDtypeStruct(sh, _F32) for sh in shapes],
        scratch_shapes=(
            [pltpu.VMEM(sh, _F32) for sh in shapes]
            + [pltpu.VMEM((n_parts, D_MODEL, D_CONV), _F32),
               pltpu.VMEM((D_MODEL, D_MODEL), _F32)]
            + [pltpu.VMEM(sh, _F32) for sh in shapes]
            + [pltpu.SemaphoreType.DMA((4 + n_parts,)), pltpu.SemaphoreType.DMA((3,))]),
        compiler_params=pltpu.CompilerParams(
            dimension_semantics=("arbitrary",),
            vmem_limit_bytes=VMEM_LIMIT_BYTES),
        name="sample_mixer",
    )(x, sc, sp, w_in, w_out, *small)


def kernel(x_prompt, x_sample, state_conv, state_pool, norm_g, w_in, conv_w, pool_w, pool_scale,
           w_out, final_g):
    assert norm_g.shape[0] == 1, "single-layer trunk"
    weights = (norm_g, w_in[0], jnp.swapaxes(conv_w, 0, 1), pool_w[0], pool_scale, w_out[0],
               final_g.reshape(1, D_MODEL))
    y_prompt, nc_p, np_p = _prompt_call(x_prompt, weights)
    y_sample, nc_s, np_s = _sample_call(x_sample, state_conv[0], jnp.swapaxes(state_pool[0], 0, 1),
                                        weights)
    return (y_prompt, y_sample, nc_p, jnp.swapaxes(np_p, 0, 1)[None], nc_s[None],
            jnp.swapaxes(np_s, 0, 1)[None])
```

```python
import jax
import jax.numpy as jnp
from jax import lax
from jax.experimental import pallas as pl
from jax.experimental.pallas import tpu as pltpu

D_MODEL = 1024
D_CONV = 512
D_POOL = 512
POOL_WINDOWS = (2, 4, 8, 16)
POOL_GC = D_POOL // len(POOL_WINDOWS)
POOL_MAX = max(POOL_WINDOWS)
CONV_W = 3
D_IN = 4 * D_CONV + 2 * D_POOL
EPS = 1e-6

SEQ_TILE = 1024
SUB_ROWS = 512
CONV_PAD = 8
POOL_PAD = 16
VMEM_LIMIT_BYTES = 56 * 1024 * 1024

_F32 = jnp.float32

_COL_B, _COL_C, _COL_U, _COL_ZA = 0, D_CONV, 2 * D_CONV, 3 * D_CONV
_COL_V, _COL_ZB = 4 * D_CONV, 4 * D_CONV + D_POOL


def _rmsnorm(x, g):
    ms = jnp.mean(x * x, axis=-1, keepdims=True)
    return (x * lax.rsqrt(ms + EPS)) * g


def _silu(z):
    return z * jax.nn.sigmoid(z)


def _dot(a, b):
    return jnp.dot(a.astype(jnp.bfloat16), b.astype(jnp.bfloat16), preferred_element_type=_F32)


def _split_proj(proj):
    return tuple(proj[:, c:c + D_CONV] for c in (_COL_B, _COL_C, _COL_U, _COL_ZA, _COL_V, _COL_ZB))


def _pool_project(pooled_groups, pw_ref):
    outs = [_dot(p, pw_ref[gi]) for gi, p in enumerate(pooled_groups)]
    return jnp.concatenate(outs, axis=-1)


def _window_sums(ext):
    assert all(b == 2 * a for a, b in zip((1,) + POOL_WINDOWS, POOL_WINDOWS)) and POOL_PAD >= POOL_MAX
    sums, width, out = ext, 1, []
    for w in POOL_WINDOWS:
        while width < w:
            sums = sums + pltpu.roll(sums, width, axis=0)
            width *= 2
        out.append(sums[POOL_PAD:, 0:POOL_GC])
        sums = sums[:, POOL_GC:]
    return out


def _prompt_kernel(x_ref, ng_ref, win_ref, cw_ref, pw_ref, ps_ref, wout_ref, fg_ref,
                   y_ref, nc_ref, np_ref, cu_buf, v_buf):
    j = pl.program_id(1)
    ts = SEQ_TILE
    head = POOL_MAX

    @pl.when(j == 0)
    def _():
        cu_buf[0:CONV_PAD, :] = jnp.zeros((CONV_PAD, D_CONV), _F32)
        v_buf[0:POOL_PAD, :] = jnp.zeros((POOL_PAD, D_POOL), _F32)

    cw = [cw_ref[k] for k in range(CONV_W)]
    pos = j * ts + 1 + lax.broadcasted_iota(jnp.int32, (head, POOL_GC), 0)

    n_sub = ts // SUB_ROWS
    xs = [x_ref[0, s * SUB_ROWS:(s + 1) * SUB_ROWS, :] for s in range(n_sub)]
    hs = [_rmsnorm(x, ng_ref[...]) for x in xs]
    vs = [_dot(h, win_ref[:, _COL_V:_COL_V + D_POOL]) for h in hs]
    zbs = [_dot(h, win_ref[:, _COL_ZB:_COL_ZB + D_POOL]) for h in hs]
    projs = [_dot(h, win_ref[:, _COL_B:_COL_ZA + D_CONV]) for h in hs]

    pooled = []
    for s, v in enumerate(vs):
        p0 = POOL_PAD + s * SUB_ROWS
        v_buf[p0:p0 + SUB_ROWS, :] = v
        wsums = _window_sums(v_buf[p0 - POOL_PAD:p0 + SUB_ROWS, :])
        groups = []
        for gi, w in enumerate(POOL_WINDOWS):
            acc = wsums[gi]
            if s == 0:
                cnt = jnp.minimum(pos, w).astype(_F32)
                mean = jnp.concatenate([acc[0:head] / cnt, acc[head:] * (1.0 / w)], axis=0)
            else:
                mean = acc * (1.0 / w)
            groups.append(mean - v[:, gi * POOL_GC:(gi + 1) * POOL_GC])
        pooled.append(groups)

    y_as = []
    for s, proj in enumerate(projs):
        b_g, c_g, u, z_a = (proj[:, c:c + D_CONV] for c in (_COL_B, _COL_C, _COL_U, _COL_ZA))
        cu = c_g * u
        c0 = CONV_PAD + s * SUB_ROWS
        cu_buf[c0:c0 + SUB_ROWS, :] = cu
        yc = cw[0] * cu_buf[c0 - 2:c0 - 2 + SUB_ROWS, :]
        yc = yc + cw[1] * cu_buf[c0 - 1:c0 - 1 + SUB_ROWS, :]
        yc = yc + cw[2] * cu
        y_as.append(b_g * yc * _silu(z_a))

    y_bs = [_pool_project(p, pw_ref) * ps_ref[...] * _silu(z_b) for p, z_b in zip(pooled, zbs)]
    mixes = [_dot(jnp.concatenate([y_a, y_b], axis=-1), wout_ref[...]) for y_a, y_b in zip(y_as, y_bs)]
    for s, (x, mix) in enumerate(zip(xs, mixes)):
        y_ref[0, s * SUB_ROWS:(s + 1) * SUB_ROWS, :] = _rmsnorm(x + mix, fg_ref[...])

    @pl.when(j == pl.num_programs(1) - 1)
    def _():
        nc_ref[0, 0] = cu_buf[CONV_PAD + ts - (CONV_W - 1):CONV_PAD + ts, :]

    hist = POOL_MAX - 1
    tail0 = POOL_PAD + ts - hist
    b = pl.program_id(0)
    n_seq = np_ref.shape[1]

    @pl.when((j == pl.num_programs(1) - 1) & (b == 0))
    def _():
        for k in range(hist):
            np_ref[k] = jnp.broadcast_to(v_buf[tail0 + k:tail0 + k + 1, :], (n_seq, D_POOL))

    @pl.when((j == pl.num_programs(1) - 1) & (b > 0))
    def _():
        mine = lax.broadcasted_iota(jnp.int32, (n_seq, D_POOL), 0) == b
        for k in range(hist):
            row = jnp.broadcast_to(v_buf[tail0 + k:tail0 + k + 1, :], (n_seq, D_POOL))
            np_ref[k] = jnp.where(mine, row, np_ref[k])

    cu_buf[0:CONV_PAD, :] = cu_buf[ts:ts + CONV_PAD, :]
    v_buf[0:POOL_PAD, :] = v_buf[ts:ts + POOL_PAD, :]


_SAMPLE_PART_COLS = (_COL_V, _COL_ZB, _COL_B, _COL_C, _COL_U, _COL_ZA)


def _sample_kernel(x_hbm, sc_hbm, sp_hbm, win_hbm, wout_hbm, ng_ref, cw_ref, pw_ref, ps_ref, fg_ref,
                   y_hbm, nc_hbm, np_hbm,
                   x_buf, sc_buf, sp_buf, win_bufs, wout_buf, y_buf, nc_buf, np_buf, in_sems, out_sems):
    n_tok, n_seq, _ = x_buf.shape
    hist = POOL_MAX - 1
    n_cs = CONV_W - 1

    x_cps = [pltpu.make_async_copy(x_hbm.at[:, t, :], x_buf.at[t], in_sems.at[t]) for t in range(n_tok)]
    sc_cps = [pltpu.make_async_copy(sc_hbm.at[:, k, :], sc_buf.at[k], in_sems.at[n_tok + k])
              for k in range(n_cs)]
    base = n_tok + n_cs
    sp_cp = pltpu.make_async_copy(sp_hbm, sp_buf, in_sems.at[base])
    wout_cp = pltpu.make_async_copy(wout_hbm, wout_buf, in_sems.at[base + 1])
    part_cps = [pltpu.make_async_copy(win_hbm.at[:, pl.ds(col, D_CONV)], win_bufs.at[i],
                                      in_sems.at[base + 2 + i])
                for i, col in enumerate(_SAMPLE_PART_COLS)]
    for cp in x_cps + [part_cps[0], sp_cp] + part_cps[1:5] + sc_cps + [part_cps[5], wout_cp]:
        cp.start()

    def tok(a, t):
        return a[t * n_seq:(t + 1) * n_seq, :]

    def part(i, h):
        part_cps[i].wait()
        return _dot(h, win_bufs[i])

    for cp in x_cps:
        cp.wait()
    x = jnp.concatenate([x_buf[t] for t in range(n_tok)], axis=0)
    h = _rmsnorm(x, ng_ref[...])

    v = part(0, h)
    sp_cp.wait()
    vp = [sp_buf[k] for k in range(hist)] + [tok(v, t) for t in range(n_tok)]
    for k in range(hist):
        np_buf[k] = vp[n_tok + k]
    np_out = pltpu.make_async_copy(np_buf, np_hbm, out_sems.at[0])
    np_out.start()
    pooled = []
    for gi, w in enumerate(POOL_WINDOWS):
        lo = gi * POOL_GC
        rows = []
        for t in range(n_tok):
            vg = vp[hist + t][:, lo:lo + POOL_GC]
            acc = vg
            for k in range(1, w):
                acc = acc + vp[hist + t - k][:, lo:lo + POOL_GC]
            rows.append(acc * (1.0 / w) - vg)
        pooled.append(jnp.concatenate(rows, axis=0))
    z_b = part(1, h)
    y_b = _pool_project(pooled, pw_ref) * ps_ref[...] * _silu(z_b)

    b_g = part(2, h)
    cu = part(3, h) * part(4, h)
    for cp in sc_cps:
        cp.wait()
    cw = [cw_ref[k] for k in range(CONV_W)]
    up = [sc_buf[k] for k in range(n_cs)] + [tok(cu, t) for t in range(n_tok)]
    nc_outs = []
    for k in range(n_cs):
        nc_buf[k] = up[n_tok + k]
        nc_outs.append(pltpu.make_async_copy(nc_buf.at[k], nc_hbm.at[:, k, :], out_sems.at[1 + k]))
        nc_outs[-1].start()
    yc = []
    for t in range(n_tok):
        acc = cw[0] * up[t]
        for k in range(1, CONV_W):
            acc = acc + cw[k] * up[t + k]
        yc.append(acc)
    y_a = b_g * jnp.concatenate(yc, axis=0) * _silu(part(5, h))

    wout_cp.wait()
    ycat = jnp.concatenate([y_a, y_b], axis=-1)
    y = _rmsnorm(x + _dot(ycat, wout_buf[...]), fg_ref[...])
    y_outs = []
    for t in range(n_tok):
        y_buf[t] = tok(y, t)
        y_outs.append(pltpu.make_async_copy(y_buf.at[t], y_hbm.at[:, t, :], out_sems.at[1 + n_cs + t]))
        y_outs[-1].start()
    np_out.wait()
    for cp in nc_outs + y_outs:
        cp.wait()


def _const_spec(shape):
    nd = len(shape)
    return pl.BlockSpec(shape, lambda *_: (0,) * nd, pipeline_mode=pl.Buffered(1))


def _weight_specs():
    return [
        _const_spec((1, D_MODEL)),
        _const_spec((D_MODEL, D_IN)),
        _const_spec((CONV_W, 1, D_CONV)),
        _const_spec((len(POOL_WINDOWS), POOL_GC, POOL_GC)),
        _const_spec((1, D_POOL)),
        _const_spec((D_MODEL, D_MODEL)),
        _const_spec((1, D_MODEL)),
    ]


def _prompt_call(x, weights):
    batch, seq, _ = x.shape
    assert seq % SEQ_TILE == 0 and SEQ_TILE % SUB_ROWS == 0
    n_tiles = seq // SEQ_TILE
    return pl.pallas_call(
        _prompt_kernel,
        grid=(batch, n_tiles),
        in_specs=[pl.BlockSpec((1, SEQ_TILE, D_MODEL), lambda b, j: (b, j, 0))] + _weight_specs(),
        out_specs=[
            pl.BlockSpec((1, SEQ_TILE, D_MODEL), lambda b, j: (b, j, 0)),
            pl.BlockSpec((1, 1, CONV_W - 1, D_CONV), lambda b, j: (0, b, 0, 0)),
            pl.BlockSpec((POOL_MAX - 1, batch, D_POOL), lambda b, j: (0, 0, 0)),
        ],
        out_shape=[
            jax.ShapeDtypeStruct((batch, seq, D_MODEL), _F32),
            jax.ShapeDtypeStruct((1, batch, CONV_W - 1, D_CONV), _F32),
            jax.ShapeDtypeStruct((POOL_MAX - 1, batch, D_POOL), _F32),
        ],
        scratch_shapes=[
            pltpu.VMEM((CONV_PAD + SEQ_TILE, D_CONV), _F32),
            pltpu.VMEM((POOL_PAD + SEQ_TILE, D_POOL), _F32),
        ],
        compiler_params=pltpu.CompilerParams(
            dimension_semantics=("arbitrary", "arbitrary"),
            vmem_limit_bytes=VMEM_LIMIT_BYTES),
        name="prompt_mixer",
    )(x, *weights)


def _sample_call(x, sc, sp, weights):
    n_seq, n_tok, _ = x.shape
    shapes = [(n_seq, n_tok, D_MODEL), (n_seq, CONV_W - 1, D_CONV), (POOL_MAX - 1, n_seq, D_POOL)]
    ng, w_in, cw, pw, ps, w_out, fg = weights
    small = [ng, cw, pw, ps, fg]
    hbm = pl.BlockSpec(memory_space=pl.ANY)
    n_parts = len(_SAMPLE_PART_COLS)
    tm_shapes = [(n_tok, n_seq, D_MODEL), (CONV_W - 1, n_seq, D_CONV), (POOL_MAX - 1, n_seq, D_POOL)]
    return pl.pallas_call(
        _sample_kernel,
        grid=(1,),
        in_specs=[hbm] * 5 + [_const_spec(a.shape) for a in small],
        out_specs=[hbm] * 3,
        out_shape=[jax.ShapeDtypeStruct(sh, _F32) for sh in shapes],
        scratch_shapes=(
            [pltpu.VMEM(sh, _F32) for sh in tm_shapes]
            + [pltpu.VMEM((n_parts, D_MODEL, D_CONV), _F32),
               pltpu.VMEM((D_MODEL, D_MODEL), _F32)]
            + [pltpu.VMEM(sh, _F32) for sh in tm_shapes]
            + [pltpu.SemaphoreType.DMA((n_tok + CONV_W - 1 + 2 + n_parts,)),
               pltpu.SemaphoreType.DMA((1 + CONV_W - 1 + n_tok,))]),
        compiler_params=pltpu.CompilerParams(
            dimension_semantics=("arbitrary",),
            vmem_limit_bytes=VMEM_LIMIT_BYTES),
        name="sample_mixer",
    )(x, sc, sp, w_in, w_out, *small)


def kernel(x_prompt, x_sample, state_conv, state_pool, norm_g, w_in, conv_w, pool_w, pool_scale,
           w_out, final_g):
    assert norm_g.shape[0] == 1, "single-layer trunk"
    weights = (norm_g, w_in[0], jnp.swapaxes(conv_w, 0, 1), pool_w[0], pool_scale, w_out[0],
               final_g.reshape(1, D_MODEL))
    y_prompt, nc_p, np_p = _prompt_call(x_prompt, weights)
    y_sample, nc_s, np_s = _sample_call(x_sample, state_conv[0], jnp.swapaxes(state_pool[0], 0, 1),
                                        weights)
    return (y_prompt, y_sample, nc_p, jnp.swapaxes(np_p, 0, 1)[None], nc_s[None],
            jnp.swapaxes(np_s, 0, 1)[None])
```

```python
import functools

import jax
import jax.numpy as jnp
from jax import lax
from jax.experimental import pallas as pl
from jax.experimental.pallas import tpu as pltpu

D_MODEL = 1024
D_CONV = 512
D_POOL = 512
POOL_WINDOWS = (2, 4, 8, 16)
POOL_GC = D_POOL // len(POOL_WINDOWS)
POOL_MAX = max(POOL_WINDOWS)
POOL_HIST = POOL_MAX - 1
CONV_W = 3
CONV_HIST = CONV_W - 1
D_IN = 4 * D_CONV + 2 * D_POOL
EPS = 1e-6

SEQ_TILE = 1024
SUB_ROWS = 512
CONV_PAD = 8
POOL_PAD = 16
VMEM_LIMIT_BYTES = 60 * 1024 * 1024

_F32 = jnp.float32

_COL_B, _COL_C, _COL_U, _COL_ZA = 0, D_CONV, 2 * D_CONV, 3 * D_CONV
_COL_V, _COL_ZB = 4 * D_CONV, 4 * D_CONV + D_POOL


def _rmsnorm(x, g):
    ms = jnp.mean(x * x, axis=-1, keepdims=True)
    return (x * lax.rsqrt(ms + EPS)) * g


def _silu(z):
    return z * jax.nn.sigmoid(z)


def _dot(a, b):
    return jnp.dot(a.astype(jnp.bfloat16), b.astype(jnp.bfloat16), preferred_element_type=_F32)


def _pool_project(pooled_groups, pw_ref):
    outs = [_dot(p, pw_ref[gi]) for gi, p in enumerate(pooled_groups)]
    return jnp.concatenate(outs, axis=-1)


def _window_sums(ext):
    assert all(b == 2 * a for a, b in zip((1,) + POOL_WINDOWS, POOL_WINDOWS)) and POOL_PAD >= POOL_MAX
    sums, width, out = ext, 1, []
    for w in POOL_WINDOWS:
        while width < w:
            sums = sums + pltpu.roll(sums, width, axis=0)
            width *= 2
        out.append(sums[POOL_PAD:, 0:POOL_GC])
        sums = sums[:, POOL_GC:]
    return out


def _prompt_tile(b, j, last_j, x_ref, weights, y_ref, nc_ref, np_ref, cu_buf, v_buf):
    ng_ref, win_ref, cw_ref, pw_ref, ps_ref, wout_ref, fg_ref = weights
    ts = SEQ_TILE
    head = POOL_MAX

    @pl.when(j == 0)
    def _():
        cu_buf[0:CONV_PAD, :] = jnp.zeros((CONV_PAD, D_CONV), _F32)
        v_buf[0:POOL_PAD, :] = jnp.zeros((POOL_PAD, D_POOL), _F32)

    cw = [cw_ref[k] for k in range(CONV_W)]
    pos = j * ts + 1 + lax.broadcasted_iota(jnp.int32, (head, POOL_GC), 0)

    n_sub = ts // SUB_ROWS
    xs = [x_ref[0, s * SUB_ROWS:(s + 1) * SUB_ROWS, :] for s in range(n_sub)]
    hs = [_rmsnorm(x, ng_ref[...]) for x in xs]
    vs = [_dot(h, win_ref[:, _COL_V:_COL_V + D_POOL]) for h in hs]
    zbs = [_dot(h, win_ref[:, _COL_ZB:_COL_ZB + D_POOL]) for h in hs]
    projs = [_dot(h, win_ref[:, _COL_B:_COL_ZA + D_CONV]) for h in hs]

    pooled = []
    for s, v in enumerate(vs):
        p0 = POOL_PAD + s * SUB_ROWS
        v_buf[p0:p0 + SUB_ROWS, :] = v
        wsums = _window_sums(v_buf[p0 - POOL_PAD:p0 + SUB_ROWS, :])
        groups = []
        for gi, w in enumerate(POOL_WINDOWS):
            acc = wsums[gi]
            if s == 0:
                cnt = jnp.minimum(pos, w).astype(_F32)
                mean = jnp.concatenate([acc[0:head] / cnt, acc[head:] * (1.0 / w)], axis=0)
            else:
                mean = acc * (1.0 / w)
            groups.append(mean - v[:, gi * POOL_GC:(gi + 1) * POOL_GC])
        pooled.append(groups)

    y_as = []
    for s, proj in enumerate(projs):
        b_g, c_g, u, z_a = (proj[:, c:c + D_CONV] for c in (_COL_B, _COL_C, _COL_U, _COL_ZA))
        cu = c_g * u
        c0 = CONV_PAD + s * SUB_ROWS
        cu_buf[c0:c0 + SUB_ROWS, :] = cu
        yc = cw[0] * cu_buf[c0 - 2:c0 - 2 + SUB_ROWS, :]
        yc = yc + cw[1] * cu_buf[c0 - 1:c0 - 1 + SUB_ROWS, :]
        yc = yc + cw[2] * cu
        y_as.append(b_g * yc * _silu(z_a))

    y_bs = [_pool_project(p, pw_ref) * ps_ref[...] * _silu(z_b) for p, z_b in zip(pooled, zbs)]
    mixes = [_dot(jnp.concatenate([y_a, y_b], axis=-1), wout_ref[...]) for y_a, y_b in zip(y_as, y_bs)]
    for s, (x, mix) in enumerate(zip(xs, mixes)):
        y_ref[0, s * SUB_ROWS:(s + 1) * SUB_ROWS, :] = _rmsnorm(x + mix, fg_ref[...])

    @pl.when(j == last_j)
    def _():
        nc_ref[0, 0] = cu_buf[CONV_PAD + ts - CONV_HIST:CONV_PAD + ts, :]

    tail0 = POOL_PAD + ts - POOL_HIST
    n_seq = np_ref.shape[1]

    @pl.when((j == last_j) & (b == 0))
    def _():
        for k in range(POOL_HIST):
            np_ref[k] = jnp.broadcast_to(v_buf[tail0 + k:tail0 + k + 1, :], (n_seq, D_POOL))

    @pl.when((j == last_j) & (b > 0))
    def _():
        mine = lax.broadcasted_iota(jnp.int32, (n_seq, D_POOL), 0) == b
        for k in range(POOL_HIST):
            row = jnp.broadcast_to(v_buf[tail0 + k:tail0 + k + 1, :], (n_seq, D_POOL))
            np_ref[k] = jnp.where(mine, row, np_ref[k])

    cu_buf[0:CONV_PAD, :] = cu_buf[ts:ts + CONV_PAD, :]
    v_buf[0:POOL_PAD, :] = v_buf[ts:ts + POOL_PAD, :]


def _sample_copies(xs_hbm, sc_hbm, sp_hbm, ys_hbm, ncs_hbm, nps_hbm,
                   xs_buf, sc_buf, sp_buf, ncs_buf, nps_buf, in_sems, out_sems):
    n_tok, n_cs = xs_buf.shape[0], sc_buf.shape[0]
    keep = POOL_HIST - n_tok
    ins = ([pltpu.make_async_copy(xs_hbm.at[:, t, :], xs_buf.at[t], in_sems.at[t]) for t in range(n_tok)]
           + [pltpu.make_async_copy(sc_hbm.at[:, k, :], sc_buf.at[k], in_sems.at[n_tok + k])
              for k in range(n_cs)]
           + [pltpu.make_async_copy(sp_hbm, sp_buf, in_sems.at[n_tok + n_cs]),
              pltpu.make_async_copy(sp_hbm.at[pl.ds(n_tok, keep)], nps_hbm.at[pl.ds(0, keep)],
                                    in_sems.at[n_tok + n_cs + 1])])
    outs = ([pltpu.make_async_copy(xs_buf.at[t], ys_hbm.at[:, t, :], out_sems.at[t]) for t in range(n_tok)]
            + [pltpu.make_async_copy(ncs_buf.at[k], ncs_hbm.at[:, k, :], out_sems.at[n_tok + k])
               for k in range(n_cs)]
            + [pltpu.make_async_copy(nps_buf, nps_hbm.at[pl.ds(keep, n_tok)], out_sems.at[n_tok + n_cs])])
    return ins, outs


def _sample_step(weights, xs_buf, sc_buf, sp_buf, ncs_buf, nps_buf):
    ng_ref, win_ref, cw_ref, pw_ref, ps_ref, wout_ref, fg_ref = weights
    n_tok, n_seq, _ = xs_buf.shape

    def tok(a, t):
        return a[t * n_seq:(t + 1) * n_seq, :]

    x = jnp.concatenate([xs_buf[t] for t in range(n_tok)], axis=0)
    h = _rmsnorm(x, ng_ref[...])
    v = _dot(h, win_ref[:, _COL_V:_COL_V + D_POOL])
    z_b = _dot(h, win_ref[:, _COL_ZB:_COL_ZB + D_POOL])
    proj = _dot(h, win_ref[:, _COL_B:_COL_ZA + D_CONV])

    vp = [sp_buf[k] for k in range(POOL_HIST)] + [tok(v, t) for t in range(n_tok)]
    for t in range(n_tok):
        nps_buf[t] = tok(v, t)
    pooled = []
    for gi, w in enumerate(POOL_WINDOWS):
        lo = gi * POOL_GC
        rows = []
        for t in range(n_tok):
            vg = vp[POOL_HIST + t][:, lo:lo + POOL_GC]
            acc = vg
            for k in range(1, w):
                acc = acc + vp[POOL_HIST + t - k][:, lo:lo + POOL_GC]
            rows.append(acc * (1.0 / w) - vg)
        pooled.append(jnp.concatenate(rows, axis=0))
    y_b = _pool_project(pooled, pw_ref) * ps_ref[...] * _silu(z_b)

    b_g, c_g, u, z_a = (proj[:, c:c + D_CONV] for c in (_COL_B, _COL_C, _COL_U, _COL_ZA))
    cu = c_g * u
    cw = [cw_ref[k] for k in range(CONV_W)]
    up = [sc_buf[k] for k in range(CONV_HIST)] + [tok(cu, t) for t in range(n_tok)]
    for k in range(CONV_HIST):
        ncs_buf[k] = up[n_tok + k]
    yc = []
    for t in range(n_tok):
        acc = cw[0] * up[t]
        for k in range(1, CONV_W):
            acc = acc + cw[k] * up[t + k]
        yc.append(acc)
    y_a = b_g * jnp.concatenate(yc, axis=0) * _silu(z_a)

    ycat = jnp.concatenate([y_a, y_b], axis=-1)
    y = _rmsnorm(x + _dot(ycat, wout_ref[...]), fg_ref[...])
    for t in range(n_tok):
        xs_buf[t] = tok(y, t)


def _mixer_kernel(x_ref, xs_hbm, sc_hbm, sp_hbm, ng_ref, win_ref, cw_ref, pw_ref, ps_ref, wout_ref, fg_ref,
                  y_ref, nc_ref, np_ref, ys_hbm, ncs_hbm, nps_hbm,
                  cu_buf, v_buf, xs_buf, sc_buf, sp_buf, ncs_buf, nps_buf, in_sems, out_sems,
                  *, tiles_per_seq):
    i = pl.program_id(0)
    n_prompt = pl.num_programs(0) - 1
    weights = (ng_ref, win_ref, cw_ref, pw_ref, ps_ref, wout_ref, fg_ref)
    ins, outs = _sample_copies(xs_hbm, sc_hbm, sp_hbm, ys_hbm, ncs_hbm, nps_hbm,
                               xs_buf, sc_buf, sp_buf, ncs_buf, nps_buf, in_sems, out_sems)

    @pl.when(i == 0)
    def _():
        for cp in ins:
            cp.start()

    @pl.when(i < n_prompt)
    def _():
        _prompt_tile(i // tiles_per_seq, i % tiles_per_seq, tiles_per_seq - 1,
                     x_ref, weights, y_ref, nc_ref, np_ref, cu_buf, v_buf)

    @pl.when(i == n_prompt)
    def _():
        for cp in ins:
            cp.wait()
        _sample_step(weights, xs_buf, sc_buf, sp_buf, ncs_buf, nps_buf)
        for cp in outs:
            cp.start()
        for cp in outs:
            cp.wait()


def _const_spec(shape):
    nd = len(shape)
    return pl.BlockSpec(shape, lambda *_: (0,) * nd, pipeline_mode=pl.Buffered(1))


def kernel(x_prompt, x_sample, state_conv, state_pool, norm_g, w_in, conv_w, pool_w, pool_scale,
           w_out, final_g):
    assert norm_g.shape[0] == 1, "single-layer trunk"
    batch, seq, _ = x_prompt.shape
    n_seq, n_tok, _ = x_sample.shape
    assert seq % SEQ_TILE == 0 and SEQ_TILE % SUB_ROWS == 0 and n_tok < POOL_HIST
    tiles_per_seq = seq // SEQ_TILE
    n_prompt = batch * tiles_per_seq

    weights = (norm_g, w_in[0], jnp.swapaxes(conv_w, 0, 1), pool_w[0], pool_scale, w_out[0],
               final_g.reshape(1, D_MODEL))

    def tile(i):
        p = jnp.minimum(i, n_prompt - 1)
        return p // tiles_per_seq, p % tiles_per_seq

    hbm = pl.BlockSpec(memory_space=pl.ANY)
    outs = pl.pallas_call(
        functools.partial(_mixer_kernel, tiles_per_seq=tiles_per_seq),
        grid=(n_prompt + 1,),
        in_specs=[pl.BlockSpec((1, SEQ_TILE, D_MODEL), lambda i: (*tile(i), 0)), hbm, hbm, hbm]
        + [_const_spec(a.shape) for a in weights],
        out_specs=[
            pl.BlockSpec((1, SEQ_TILE, D_MODEL), lambda i: (*tile(i), 0)),
            pl.BlockSpec((1, 1, CONV_HIST, D_CONV), lambda i: (0, tile(i)[0], 0, 0)),
            pl.BlockSpec((POOL_HIST, batch, D_POOL), lambda i: (0, 0, 0)),
            hbm, hbm, hbm,
        ],
        out_shape=[
            jax.ShapeDtypeStruct((batch, seq, D_MODEL), _F32),
            jax.ShapeDtypeStruct((1, batch, CONV_HIST, D_CONV), _F32),
            jax.ShapeDtypeStruct((POOL_HIST, batch, D_POOL), _F32),
            jax.ShapeDtypeStruct((n_seq, n_tok, D_MODEL), _F32),
            jax.ShapeDtypeStruct((n_seq, CONV_HIST, D_CONV), _F32),
            jax.ShapeDtypeStruct((POOL_HIST, n_seq, D_POOL), _F32),
        ],
        scratch_shapes=[
            pltpu.VMEM((CONV_PAD + SEQ_TILE, D_CONV), _F32),
            pltpu.VMEM((POOL_PAD + SEQ_TILE, D_POOL), _F32),
            pltpu.VMEM((n_tok, n_seq, D_MODEL), _F32),
            pltpu.VMEM((CONV_HIST, n_seq, D_CONV), _F32),
            pltpu.VMEM((POOL_HIST, n_seq, D_POOL), _F32),
            pltpu.VMEM((CONV_HIST, n_seq, D_CONV), _F32),
            pltpu.VMEM((n_tok, n_seq, D_POOL), _F32),
            pltpu.SemaphoreType.DMA((n_tok + CONV_HIST + 2,)),
            pltpu.SemaphoreType.DMA((n_tok + CONV_HIST + 1,)),
        ],
        compiler_params=pltpu.CompilerParams(
            dimension_semantics=("arbitrary",),
            vmem_limit_bytes=VMEM_LIMIT_BYTES),
        name="mixer_step",
    )(x_prompt, x_sample, state_conv[0], jnp.swapaxes(state_pool[0], 0, 1), *weights)
    y_prompt, nc_p, np_p, y_sample, nc_s, np_s = outs
    return (y_prompt, y_sample, nc_p, jnp.swapaxes(np_p, 0, 1)[None], nc_s[None],
            jnp.swapaxes(np_s, 0, 1)[None])
```

```python
import functools

import jax
import jax.numpy as jnp
from jax import lax
from jax.experimental import pallas as pl
from jax.experimental.pallas import tpu as pltpu

D_MODEL = 1024
D_CONV = 512
D_POOL = 512
POOL_WINDOWS = (2, 4, 8, 16)
POOL_GC = D_POOL // len(POOL_WINDOWS)
POOL_MAX = max(POOL_WINDOWS)
POOL_HIST = POOL_MAX - 1
CONV_W = 3
CONV_HIST = CONV_W - 1
D_IN = 4 * D_CONV + 2 * D_POOL
EPS = 1e-6

SEQ_TILE = 1024
SUB_ROWS = 512
CONV_PAD = 8
POOL_PAD = 16
VMEM_LIMIT_BYTES = 60 * 1024 * 1024

_F32 = jnp.float32

_COL_B, _COL_C, _COL_U, _COL_ZA = 0, D_CONV, 2 * D_CONV, 3 * D_CONV
_COL_V, _COL_ZB = 4 * D_CONV, 4 * D_CONV + D_POOL


def _rmsnorm(x, g):
    ms = jnp.mean(x * x, axis=-1, keepdims=True)
    return (x * lax.rsqrt(ms + EPS)) * g


def _silu(z):
    return z * jax.nn.sigmoid(z)


def _dot(a, b):
    return jnp.dot(a.astype(jnp.bfloat16), b.astype(jnp.bfloat16), preferred_element_type=_F32)


def _pool_project(pooled_groups, pw_ref):
    outs = [_dot(p, pw_ref[gi]) for gi, p in enumerate(pooled_groups)]
    return jnp.concatenate(outs, axis=-1)


def _window_sums(ext):
    assert all(b == 2 * a for a, b in zip((1,) + POOL_WINDOWS, POOL_WINDOWS)) and POOL_PAD >= POOL_MAX
    sums, width, out = ext, 1, []
    for w in POOL_WINDOWS:
        while width < w:
            sums = sums + pltpu.roll(sums, width, axis=0)
            width *= 2
        out.append(sums[POOL_PAD:, 0:POOL_GC])
        sums = sums[:, POOL_GC:]
    return out


def _prompt_tile(b, j, last_j, x_ref, weights, y_ref, nc_ref, np_ref, cu_buf, v_buf, weight_copies=None):
    def arrive(k):
        if weight_copies is not None:
            weight_copies[k].wait()

    ng_ref, win_ref, cw_ref, pw_ref, ps_ref, wout_ref, fg_ref = weights
    ts = SEQ_TILE
    head = POOL_MAX

    @pl.when(j == 0)
    def _():
        cu_buf[0:CONV_PAD, :] = jnp.zeros((CONV_PAD, D_CONV), _F32)
        v_buf[0:POOL_PAD, :] = jnp.zeros((POOL_PAD, D_POOL), _F32)

    cw = [cw_ref[k] for k in range(CONV_W)]
    pos = j * ts + 1 + lax.broadcasted_iota(jnp.int32, (head, POOL_GC), 0)

    n_sub = ts // SUB_ROWS
    xs = [x_ref[0, s * SUB_ROWS:(s + 1) * SUB_ROWS, :] for s in range(n_sub)]
    hs = [_rmsnorm(x, ng_ref[...]) for x in xs]
    arrive(0)
    vs = [_dot(h, win_ref[:, _COL_V:_COL_V + D_POOL]) for h in hs]
    arrive(1)
    zbs = [_dot(h, win_ref[:, _COL_ZB:_COL_ZB + D_POOL]) for h in hs]
    arrive(2)
    projs = [_dot(h, win_ref[:, _COL_B:_COL_ZA + D_CONV]) for h in hs]

    pooled = []
    for s, v in enumerate(vs):
        p0 = POOL_PAD + s * SUB_ROWS
        v_buf[p0:p0 + SUB_ROWS, :] = v
        wsums = _window_sums(v_buf[p0 - POOL_PAD:p0 + SUB_ROWS, :])
        groups = []
        for gi, w in enumerate(POOL_WINDOWS):
            acc = wsums[gi]
            if s == 0:
                cnt = jnp.minimum(pos, w).astype(_F32)
                mean = jnp.concatenate([acc[0:head] / cnt, acc[head:] * (1.0 / w)], axis=0)
            else:
                mean = acc * (1.0 / w)
            groups.append(mean - v[:, gi * POOL_GC:(gi + 1) * POOL_GC])
        pooled.append(groups)

    y_as = []
    for s, proj in enumerate(projs):
        b_g, c_g, u, z_a = (proj[:, c:c + D_CONV] for c in (_COL_B, _COL_C, _COL_U, _COL_ZA))
        cu = c_g * u
        c0 = CONV_PAD + s * SUB_ROWS
        cu_buf[c0:c0 + SUB_ROWS, :] = cu
        yc = cw[0] * cu_buf[c0 - 2:c0 - 2 + SUB_ROWS, :]
        yc = yc + cw[1] * cu_buf[c0 - 1:c0 - 1 + SUB_ROWS, :]
        yc = yc + cw[2] * cu
        y_as.append(b_g * yc * _silu(z_a))

    y_bs = [_pool_project(p, pw_ref) * ps_ref[...] * _silu(z_b) for p, z_b in zip(pooled, zbs)]
    arrive(3)
    mixes = [_dot(jnp.concatenate([y_a, y_b], axis=-1), wout_ref[...]) for y_a, y_b in zip(y_as, y_bs)]
    for s, (x, mix) in enumerate(zip(xs, mixes)):
        y_ref[0, s * SUB_ROWS:(s + 1) * SUB_ROWS, :] = _rmsnorm(x + mix, fg_ref[...])

    @pl.when(j == last_j)
    def _():
        nc_ref[0, 0] = cu_buf[CONV_PAD + ts - CONV_HIST:CONV_PAD + ts, :]

    tail0 = POOL_PAD + ts - POOL_HIST
    n_seq = np_ref.shape[1]

    @pl.when((j == last_j) & (b == 0))
    def _():
        for k in range(POOL_HIST):
            np_ref[k] = jnp.broadcast_to(v_buf[tail0 + k:tail0 + k + 1, :], (n_seq, D_POOL))

    @pl.when((j == last_j) & (b > 0))
    def _():
        mine = lax.broadcasted_iota(jnp.int32, (n_seq, D_POOL), 0) == b
        for k in range(POOL_HIST):
            row = jnp.broadcast_to(v_buf[tail0 + k:tail0 + k + 1, :], (n_seq, D_POOL))
            np_ref[k] = jnp.where(mine, row, np_ref[k])

    cu_buf[0:CONV_PAD, :] = cu_buf[ts:ts + CONV_PAD, :]
    v_buf[0:POOL_PAD, :] = v_buf[ts:ts + POOL_PAD, :]


def _sample_copies(xs_hbm, sc_hbm, sp_hbm, ys_hbm, ncs_hbm, nps_hbm,
                   xs_buf, sc_buf, sp_buf, ncs_buf, nps_buf, in_sems, out_sems):
    n_tok, n_cs = xs_buf.shape[0], sc_buf.shape[0]
    keep = POOL_HIST - n_tok
    ins = ([pltpu.make_async_copy(xs_hbm.at[:, t, :], xs_buf.at[t], in_sems.at[t]) for t in range(n_tok)]
           + [pltpu.make_async_copy(sc_hbm.at[:, k, :], sc_buf.at[k], in_sems.at[n_tok + k])
              for k in range(n_cs)]
           + [pltpu.make_async_copy(sp_hbm, sp_buf, in_sems.at[n_tok + n_cs]),
              pltpu.make_async_copy(sp_hbm.at[pl.ds(n_tok, keep)], nps_hbm.at[pl.ds(0, keep)],
                                    in_sems.at[n_tok + n_cs + 1])])
    outs = ([pltpu.make_async_copy(xs_buf.at[t], ys_hbm.at[:, t, :], out_sems.at[t]) for t in range(n_tok)]
            + [pltpu.make_async_copy(ncs_buf.at[k], ncs_hbm.at[:, k, :], out_sems.at[n_tok + k])
               for k in range(n_cs)]
            + [pltpu.make_async_copy(nps_buf, nps_hbm.at[pl.ds(keep, n_tok)], out_sems.at[n_tok + n_cs])])
    return ins, outs


def _sample_step(weights, xs_buf, sc_buf, sp_buf, ncs_buf, nps_buf):
    ng_ref, win_ref, cw_ref, pw_ref, ps_ref, wout_ref, fg_ref = weights
    n_tok, n_seq, _ = xs_buf.shape

    def tok(a, t):
        return a[t * n_seq:(t + 1) * n_seq, :]

    x = jnp.concatenate([xs_buf[t] for t in range(n_tok)], axis=0)
    h = _rmsnorm(x, ng_ref[...])
    v = _dot(h, win_ref[:, _COL_V:_COL_V + D_POOL])
    z_b = _dot(h, win_ref[:, _COL_ZB:_COL_ZB + D_POOL])
    proj = _dot(h, win_ref[:, _COL_B:_COL_ZA + D_CONV])

    vp = [sp_buf[k] for k in range(POOL_HIST)] + [tok(v, t) for t in range(n_tok)]
    for t in range(n_tok):
        nps_buf[t] = tok(v, t)
    pooled = []
    for gi, w in enumerate(POOL_WINDOWS):
        lo = gi * POOL_GC
        rows = []
        for t in range(n_tok):
            vg = vp[POOL_HIST + t][:, lo:lo + POOL_GC]
            acc = vg
            for k in range(1, w):
                acc = acc + vp[POOL_HIST + t - k][:, lo:lo + POOL_GC]
            rows.append(acc * (1.0 / w) - vg)
        pooled.append(jnp.concatenate(rows, axis=0))
    y_b = _pool_project(pooled, pw_ref) * ps_ref[...] * _silu(z_b)

    b_g, c_g, u, z_a = (proj[:, c:c + D_CONV] for c in (_COL_B, _COL_C, _COL_U, _COL_ZA))
    cu = c_g * u
    cw = [cw_ref[k] for k in range(CONV_W)]
    up = [sc_buf[k] for k in range(CONV_HIST)] + [tok(cu, t) for t in range(n_tok)]
    for k in range(CONV_HIST):
        ncs_buf[k] = up[n_tok + k]
    yc = []
    for t in range(n_tok):
        acc = cw[0] * up[t]
        for k in range(1, CONV_W):
            acc = acc + cw[k] * up[t + k]
        yc.append(acc)
    y_a = b_g * jnp.concatenate(yc, axis=0) * _silu(z_a)

    ycat = jnp.concatenate([y_a, y_b], axis=-1)
    y = _rmsnorm(x + _dot(ycat, wout_ref[...]), fg_ref[...])
    for t in range(n_tok):
        xs_buf[t] = tok(y, t)


def _mixer_kernel(x_ref, xs_hbm, sc_hbm, sp_hbm, win_hbm, wout_hbm, ng_ref, cw_ref, pw_ref, ps_ref, fg_ref,
                  y_ref, nc_ref, np_ref, ys_hbm, ncs_hbm, nps_hbm,
                  win_ref, wout_ref, cu_buf, v_buf, xs_buf, sc_buf, sp_buf, ncs_buf, nps_buf,
                  w_sems, in_sems, out_sems, *, tiles_per_seq):
    i = pl.program_id(0)
    n_prompt = pl.num_programs(0) - 1
    weights = (ng_ref, win_ref, cw_ref, pw_ref, ps_ref, wout_ref, fg_ref)
    w_cols = ((_COL_V, D_POOL), (_COL_ZB, D_POOL), (_COL_B, 4 * D_CONV))
    weight_copies = [pltpu.make_async_copy(win_hbm.at[:, pl.ds(c, n)], win_ref.at[:, pl.ds(c, n)],
                                           w_sems.at[k]) for k, (c, n) in enumerate(w_cols)]
    weight_copies.append(pltpu.make_async_copy(wout_hbm, wout_ref, w_sems.at[len(w_cols)]))
    ins, outs = _sample_copies(xs_hbm, sc_hbm, sp_hbm, ys_hbm, ncs_hbm, nps_hbm,
                               xs_buf, sc_buf, sp_buf, ncs_buf, nps_buf, in_sems, out_sems)

    @pl.when(i == 0)
    def _():
        for cp in weight_copies + ins:
            cp.start()
        _prompt_tile(0, 0, tiles_per_seq - 1, x_ref, weights, y_ref, nc_ref, np_ref, cu_buf, v_buf,
                     weight_copies)

    @pl.when((i > 0) & (i < n_prompt))
    def _():
        _prompt_tile(i // tiles_per_seq, i % tiles_per_seq, tiles_per_seq - 1,
                     x_ref, weights, y_ref, nc_ref, np_ref, cu_buf, v_buf)

    @pl.when(i == n_prompt)
    def _():
        for cp in ins:
            cp.wait()
        _sample_step(weights, xs_buf, sc_buf, sp_buf, ncs_buf, nps_buf)
        for cp in outs:
            cp.start()
        for cp in outs:
            cp.wait()


def _const_spec(shape):
    nd = len(shape)
    return pl.BlockSpec(shape, lambda *_: (0,) * nd, pipeline_mode=pl.Buffered(1))


def kernel(x_prompt, x_sample, state_conv, state_pool, norm_g, w_in, conv_w, pool_w, pool_scale,
           w_out, final_g):
    assert norm_g.shape[0] == 1, "single-layer trunk"
    batch, seq, _ = x_prompt.shape
    n_seq, n_tok, _ = x_sample.shape
    assert seq % SEQ_TILE == 0 and SEQ_TILE % SUB_ROWS == 0 and n_tok < POOL_HIST
    tiles_per_seq = seq // SEQ_TILE
    n_prompt = batch * tiles_per_seq

    small = (norm_g, jnp.swapaxes(conv_w, 0, 1), pool_w[0], pool_scale, final_g.reshape(1, D_MODEL))

    def tile(i):
        p = jnp.minimum(i, n_prompt - 1)
        return p // tiles_per_seq, p % tiles_per_seq

    hbm = pl.BlockSpec(memory_space=pl.ANY)
    outs = pl.pallas_call(
        functools.partial(_mixer_kernel, tiles_per_seq=tiles_per_seq),
        grid=(n_prompt + 1,),
        in_specs=[pl.BlockSpec((1, SEQ_TILE, D_MODEL), lambda i: (*tile(i), 0))] + [hbm] * 5
        + [_const_spec(a.shape) for a in small],
        out_specs=[
            pl.BlockSpec((1, SEQ_TILE, D_MODEL), lambda i: (*tile(i), 0)),
            pl.BlockSpec((1, 1, CONV_HIST, D_CONV), lambda i: (0, tile(i)[0], 0, 0)),
            pl.BlockSpec((POOL_HIST, batch, D_POOL), lambda i: (0, 0, 0)),
            hbm, hbm, hbm,
        ],
        out_shape=[
            jax.ShapeDtypeStruct((batch, seq, D_MODEL), _F32),
            jax.ShapeDtypeStruct((1, batch, CONV_HIST, D_CONV), _F32),
            jax.ShapeDtypeStruct((POOL_HIST, batch, D_POOL), _F32),
            jax.ShapeDtypeStruct((n_seq, n_tok, D_MODEL), _F32),
            jax.ShapeDtypeStruct((n_seq, CONV_HIST, D_CONV), _F32),
            jax.ShapeDtypeStruct((POOL_HIST, n_seq, D_POOL), _F32),
        ],
        scratch_shapes=[
            pltpu.VMEM((D_MODEL, D_IN), _F32),
            pltpu.VMEM((D_MODEL, D_MODEL), _F32),
            pltpu.VMEM((CONV_PAD + SEQ_TILE, D_CONV), _F32),
            pltpu.VMEM((POOL_PAD + SEQ_TILE, D_POOL), _F32),
            pltpu.VMEM((n_tok, n_seq, D_MODEL), _F32),
            pltpu.VMEM((CONV_HIST, n_seq, D_CONV), _F32),
            pltpu.VMEM((POOL_HIST, n_seq, D_POOL), _F32),
            pltpu.VMEM((CONV_HIST, n_seq, D_CONV), _F32),
            pltpu.VMEM((n_tok, n_seq, D_POOL), _F32),
            pltpu.SemaphoreType.DMA((4,)),
            pltpu.SemaphoreType.DMA((n_tok + CONV_HIST + 2,)),
            pltpu.SemaphoreType.DMA((n_tok + CONV_HIST + 1,)),
        ],
        compiler_params=pltpu.CompilerParams(
            dimension_semantics=("arbitrary",),
            vmem_limit_bytes=VMEM_LIMIT_BYTES),
        name="mixer_step",
    )(x_prompt, x_sample, state_conv[0], jnp.swapaxes(state_pool[0], 0, 1), w_in[0], w_out[0], *small)
    y_prompt, nc_p, np_p, y_sample, nc_s, np_s = outs
    return (y_prompt, y_sample, nc_p, jnp.swapaxes(np_p, 0, 1)[None], nc_s[None],
            jnp.swapaxes(np_s, 0, 1)[None])
```

```python
import functools

import jax
import jax.numpy as jnp
from jax import lax
from jax.experimental import pallas as pl
from jax.experimental.pallas import tpu as pltpu

D_MODEL = 1024
D_CONV = 512
D_POOL = 512
POOL_WINDOWS = (2, 4, 8, 16)
POOL_GC = D_POOL // len(POOL_WINDOWS)
POOL_MAX = max(POOL_WINDOWS)
POOL_HIST = POOL_MAX - 1
CONV_W = 3
CONV_HIST = CONV_W - 1
D_IN = 4 * D_CONV + 2 * D_POOL
EPS = 1e-6

SEQ_TILE = 1024
SUB_TILES = ((0, 256), (256, 256), (512, 256), (768, 256))
BF16_TILE_ROWS = 16
CONV_PAD = 8
POOL_PAD = 16
VMEM_LIMIT_BYTES = 60 * 1024 * 1024

_F32 = jnp.float32

_COL_B, _COL_C, _COL_U, _COL_ZA = 0, D_CONV, 2 * D_CONV, 3 * D_CONV
_COL_V, _COL_ZB = 4 * D_CONV, 4 * D_CONV + D_POOL


def _rmsnorm(x, g):
    ms = jnp.mean(x * x, axis=-1, keepdims=True)
    return (x * lax.rsqrt(ms + EPS)) * g


def _silu(z):
    return z * jax.nn.sigmoid(z)


def _dot(a, b):
    return jnp.dot(a.astype(jnp.bfloat16), b.astype(jnp.bfloat16), preferred_element_type=_F32)


def _pool_project(pooled_groups, pw_ref):
    outs = [_dot(p, pw_ref[gi]) for gi, p in enumerate(pooled_groups)]
    return jnp.concatenate(outs, axis=-1)


def _window_sums(ext):
    assert all(b == 2 * a for a, b in zip((1,) + POOL_WINDOWS, POOL_WINDOWS)) and POOL_PAD >= POOL_MAX
    sums, width, out = ext, 1, []
    for w in POOL_WINDOWS:
        while width < w:
            sums = sums + pltpu.roll(sums, width, axis=0)
            width *= 2
        out.append(sums[POOL_PAD:, 0:POOL_GC])
        sums = sums[:, POOL_GC:]
    return out


def _prompt_tile(b, j, last_j, x_ref, weights, y_ref, nc_ref, np_ref, cu_buf, v_buf, weight_copies=None):
    def arrive(k):
        if weight_copies is not None:
            weight_copies[k].wait()

    ng_ref, win_ref, cw_ref, pw_ref, ps_ref, wout_ref, fg_ref = weights
    ts = SEQ_TILE
    head = POOL_MAX

    @pl.when(j == 0)
    def _():
        cu_buf[0:CONV_PAD, :] = jnp.zeros((CONV_PAD, D_CONV), _F32)
        v_buf[0:POOL_PAD, :] = jnp.zeros((POOL_PAD, D_POOL), _F32)

    cw = [cw_ref[k] for k in range(CONV_W)]
    pos = j * ts + 1 + lax.broadcasted_iota(jnp.int32, (head, POOL_GC), 0)

    xs = [x_ref[0, r0:r0 + n, :] for r0, n in SUB_TILES]
    hs = [_rmsnorm(x, ng_ref[...]) for x in xs]
    arrive(0)
    vs = [_dot(h, win_ref[:, _COL_V:_COL_V + D_POOL]) for h in hs]
    arrive(1)
    zbs = [_dot(h, win_ref[:, _COL_ZB:_COL_ZB + D_POOL]) for h in hs]
    arrive(2)
    projs = [_dot(h, win_ref[:, _COL_B:_COL_ZA + D_CONV]) for h in hs]

    pooled = []
    for s, v in enumerate(vs):
        r0, n = SUB_TILES[s]
        p0 = POOL_PAD + r0
        v_buf[p0:p0 + n, :] = v
        wsums = _window_sums(v_buf[p0 - POOL_PAD:p0 + n, :])
        groups = []
        for gi, w in enumerate(POOL_WINDOWS):
            acc = wsums[gi]
            if s == 0:
                cnt = jnp.minimum(pos, w).astype(_F32)
                mean = jnp.concatenate([acc[0:head] / cnt, acc[head:] * (1.0 / w)], axis=0)
            else:
                mean = acc * (1.0 / w)
            groups.append(mean - v[:, gi * POOL_GC:(gi + 1) * POOL_GC])
        pooled.append(groups)

    y_as = []
    for s, proj in enumerate(projs):
        b_g, c_g, u, z_a = (proj[:, c:c + D_CONV] for c in (_COL_B, _COL_C, _COL_U, _COL_ZA))
        cu = c_g * u
        r0, n = SUB_TILES[s]
        c0 = CONV_PAD + r0
        cu_buf[c0:c0 + n, :] = cu
        yc = cw[0] * cu_buf[c0 - 2:c0 - 2 + n, :]
        yc = yc + cw[1] * cu_buf[c0 - 1:c0 - 1 + n, :]
        yc = yc + cw[2] * cu
        y_as.append(b_g * yc * _silu(z_a))

    y_bs = [_pool_project(p, pw_ref) * ps_ref[...] * _silu(z_b) for p, z_b in zip(pooled, zbs)]
    arrive(3)
    mixes = [_dot(jnp.concatenate([y_a, y_b], axis=-1), wout_ref[...]) for y_a, y_b in zip(y_as, y_bs)]
    for (r0, n), x, mix in zip(SUB_TILES, xs, mixes):
        y_ref[0, r0:r0 + n, :] = _rmsnorm(x + mix, fg_ref[...])

    @pl.when(j == last_j)
    def _():
        nc_ref[0, 0] = cu_buf[CONV_PAD + ts - CONV_HIST:CONV_PAD + ts, :]

    tail0 = POOL_PAD + ts - POOL_HIST
    n_seq = np_ref.shape[1]

    @pl.when((j == last_j) & (b == 0))
    def _():
        for k in range(POOL_HIST):
            np_ref[k] = jnp.broadcast_to(v_buf[tail0 + k:tail0 + k + 1, :], (n_seq, D_POOL))

    @pl.when((j == last_j) & (b > 0))
    def _():
        mine = lax.broadcasted_iota(jnp.int32, (n_seq, D_POOL), 0) == b
        for k in range(POOL_HIST):
            row = jnp.broadcast_to(v_buf[tail0 + k:tail0 + k + 1, :], (n_seq, D_POOL))
            np_ref[k] = jnp.where(mine, row, np_ref[k])

    cu_buf[0:CONV_PAD, :] = cu_buf[ts:ts + CONV_PAD, :]
    v_buf[0:POOL_PAD, :] = v_buf[ts:ts + POOL_PAD, :]


def _sample_copies(xs_hbm, sc_hbm, sp_hbm, ys_hbm, ncs_hbm, nps_hbm,
                   xs_buf, sc_buf, sp_buf, ncs_buf, nps_buf, in_sems, out_sems):
    n_tok, n_cs = xs_buf.shape[0], sc_buf.shape[0]
    keep = POOL_HIST - n_tok
    ins = ([pltpu.make_async_copy(xs_hbm.at[:, t, :], xs_buf.at[t], in_sems.at[t]) for t in range(n_tok)]
           + [pltpu.make_async_copy(sc_hbm.at[:, k, :], sc_buf.at[k], in_sems.at[n_tok + k])
              for k in range(n_cs)]
           + [pltpu.make_async_copy(sp_hbm, sp_buf, in_sems.at[n_tok + n_cs]),
              pltpu.make_async_copy(sp_hbm.at[pl.ds(n_tok, keep)], nps_hbm.at[pl.ds(0, keep)],
                                    in_sems.at[n_tok + n_cs + 1])])
    outs = ([pltpu.make_async_copy(xs_buf.at[t], ys_hbm.at[:, t, :], out_sems.at[t]) for t in range(n_tok)]
            + [pltpu.make_async_copy(ncs_buf.at[k], ncs_hbm.at[:, k, :], out_sems.at[n_tok + k])
               for k in range(n_cs)]
            + [pltpu.make_async_copy(nps_buf, nps_hbm.at[pl.ds(keep, n_tok)], out_sems.at[n_tok + n_cs])])
    return ins, outs


def _sample_step(weights, xs_buf, sc_buf, sp_buf, ncs_buf, nps_buf):
    ng_ref, win_ref, cw_ref, pw_ref, ps_ref, wout_ref, fg_ref = weights
    n_tok, n_seq, _ = xs_buf.shape

    def tok(a, t):
        return a[t * n_seq:(t + 1) * n_seq, :]

    x = jnp.concatenate([xs_buf[t] for t in range(n_tok)], axis=0)
    h = _rmsnorm(x, ng_ref[...])
    v = _dot(h, win_ref[:, _COL_V:_COL_V + D_POOL])
    z_b = _dot(h, win_ref[:, _COL_ZB:_COL_ZB + D_POOL])
    proj = _dot(h, win_ref[:, _COL_B:_COL_ZA + D_CONV])

    vp = [sp_buf[k] for k in range(POOL_HIST)] + [tok(v, t) for t in range(n_tok)]
    for t in range(n_tok):
        nps_buf[t] = tok(v, t)
    pooled = []
    for gi, w in enumerate(POOL_WINDOWS):
        lo = gi * POOL_GC
        rows = []
        for t in range(n_tok):
            vg = vp[POOL_HIST + t][:, lo:lo + POOL_GC]
            acc = vg
            for k in range(1, w):
                acc = acc + vp[POOL_HIST + t - k][:, lo:lo + POOL_GC]
            rows.append(acc * (1.0 / w) - vg)
        pooled.append(jnp.concatenate(rows, axis=0))
    y_b = _pool_project(pooled, pw_ref) * ps_ref[...] * _silu(z_b)

    b_g, c_g, u, z_a = (proj[:, c:c + D_CONV] for c in (_COL_B, _COL_C, _COL_U, _COL_ZA))
    cu = c_g * u
    cw = [cw_ref[k] for k in range(CONV_W)]
    up = [sc_buf[k] for k in range(CONV_HIST)] + [tok(cu, t) for t in range(n_tok)]
    for k in range(CONV_HIST):
        ncs_buf[k] = up[n_tok + k]
    yc = []
    for t in range(n_tok):
        acc = cw[0] * up[t]
        for k in range(1, CONV_W):
            acc = acc + cw[k] * up[t + k]
        yc.append(acc)
    y_a = b_g * jnp.concatenate(yc, axis=0) * _silu(z_a)

    ycat = jnp.concatenate([y_a, y_b], axis=-1)
    y = _rmsnorm(x + _dot(ycat, wout_ref[...]), fg_ref[...])
    for t in range(n_tok):
        xs_buf[t] = tok(y, t)


def _mixer_kernel(x_ref, xs_hbm, sc_hbm, sp_hbm, win_hbm, wout_hbm, ng_ref, cw_ref, pw_ref, ps_ref, fg_ref,
                  y_ref, nc_ref, np_ref, ys_hbm, ncs_hbm, nps_hbm,
                  win_ref, wout_ref, cu_buf, v_buf, xs_buf, sc_buf, sp_buf, ncs_buf, nps_buf,
                  w_sems, in_sems, out_sems, *, tiles_per_seq):
    i = pl.program_id(0)
    n_prompt = pl.num_programs(0) - 1
    weights = (ng_ref, win_ref, cw_ref, pw_ref, ps_ref, wout_ref, fg_ref)
    w_cols = ((_COL_V, D_POOL), (_COL_ZB, D_POOL), (_COL_B, 4 * D_CONV))
    weight_copies = [pltpu.make_async_copy(win_hbm.at[:, pl.ds(c, n)], win_ref.at[:, pl.ds(c, n)],
                                           w_sems.at[k]) for k, (c, n) in enumerate(w_cols)]
    weight_copies.append(pltpu.make_async_copy(wout_hbm, wout_ref, w_sems.at[len(w_cols)]))
    ins, outs = _sample_copies(xs_hbm, sc_hbm, sp_hbm, ys_hbm, ncs_hbm, nps_hbm,
                               xs_buf, sc_buf, sp_buf, ncs_buf, nps_buf, in_sems, out_sems)

    @pl.when(i == 0)
    def _():
        for cp in weight_copies + ins:
            cp.start()
        _prompt_tile(0, 0, tiles_per_seq - 1, x_ref, weights, y_ref, nc_ref, np_ref, cu_buf, v_buf,
                     weight_copies)

    @pl.when((i > 0) & (i < n_prompt))
    def _():
        _prompt_tile(i // tiles_per_seq, i % tiles_per_seq, tiles_per_seq - 1,
                     x_ref, weights, y_ref, nc_ref, np_ref, cu_buf, v_buf)

    @pl.when(i == n_prompt)
    def _():
        for cp in ins:
            cp.wait()
        _sample_step(weights, xs_buf, sc_buf, sp_buf, ncs_buf, nps_buf)
        for cp in outs:
            cp.start()
        for cp in outs:
            cp.wait()


def _const_spec(shape):
    nd = len(shape)
    return pl.BlockSpec(shape, lambda *_: (0,) * nd, pipeline_mode=pl.Buffered(1))


def kernel(x_prompt, x_sample, state_conv, state_pool, norm_g, w_in, conv_w, pool_w, pool_scale,
           w_out, final_g):
    assert norm_g.shape[0] == 1, "single-layer trunk"
    batch, seq, _ = x_prompt.shape
    n_seq, n_tok, _ = x_sample.shape
    assert seq % SEQ_TILE == 0 and n_tok < POOL_HIST
    assert SUB_TILES[0][0] == 0 and sum(n for _, n in SUB_TILES) == SEQ_TILE
    assert all(a + n == b for (a, n), (b, _) in zip(SUB_TILES, SUB_TILES[1:]))
    assert all(n % BF16_TILE_ROWS == 0 and n >= POOL_MAX for _, n in SUB_TILES)
    tiles_per_seq = seq // SEQ_TILE
    n_prompt = batch * tiles_per_seq

    small = (norm_g, jnp.swapaxes(conv_w, 0, 1), pool_w[0], pool_scale, final_g.reshape(1, D_MODEL))

    def tile(i):
        p = jnp.minimum(i, n_prompt - 1)
        return p // tiles_per_seq, p % tiles_per_seq

    hbm = pl.BlockSpec(memory_space=pl.ANY)
    outs = pl.pallas_call(
        functools.partial(_mixer_kernel, tiles_per_seq=tiles_per_seq),
        grid=(n_prompt + 1,),
        in_specs=[pl.BlockSpec((1, SEQ_TILE, D_MODEL), lambda i: (*tile(i), 0))] + [hbm] * 5
        + [_const_spec(a.shape) for a in small],
        out_specs=[
            pl.BlockSpec((1, SEQ_TILE, D_MODEL), lambda i: (*tile(i), 0)),
            pl.BlockSpec((1, 1, CONV_HIST, D_CONV), lambda i: (0, tile(i)[0], 0, 0)),
            pl.BlockSpec((POOL_HIST, batch, D_POOL), lambda i: (0, 0, 0)),
            hbm, hbm, hbm,
        ],
        out_shape=[
            jax.ShapeDtypeStruct((batch, seq, D_MODEL), _F32),
            jax.ShapeDtypeStruct((1, batch, CONV_HIST, D_CONV), _F32),
            jax.ShapeDtypeStruct((POOL_HIST, batch, D_POOL), _F32),
            jax.ShapeDtypeStruct((n_seq, n_tok, D_MODEL), _F32),
            jax.ShapeDtypeStruct((n_seq, CONV_HIST, D_CONV), _F32),
            jax.ShapeDtypeStruct((POOL_HIST, n_seq, D_POOL), _F32),
        ],
        scratch_shapes=[
            pltpu.VMEM((D_MODEL, D_IN), _F32),
            pltpu.VMEM((D_MODEL, D_MODEL), _F32),
            pltpu.VMEM((CONV_PAD + SEQ_TILE, D_CONV), _F32),
            pltpu.VMEM((POOL_PAD + SEQ_TILE, D_POOL), _F32),
            pltpu.VMEM((n_tok, n_seq, D_MODEL), _F32),
            pltpu.VMEM((CONV_HIST, n_seq, D_CONV), _F32),
            pltpu.VMEM((POOL_HIST, n_seq, D_POOL), _F32),
            pltpu.VMEM((CONV_HIST, n_seq, D_CONV), _F32),
            pltpu.VMEM((n_tok, n_seq, D_POOL), _F32),
            pltpu.SemaphoreType.DMA((4,)),
            pltpu.SemaphoreType.DMA((n_tok + CONV_HIST + 2,)),
            pltpu.SemaphoreType.DMA((n_tok + CONV_HIST + 1,)),
        ],
        compiler_params=pltpu.CompilerParams(
            dimension_semantics=("arbitrary",),
            vmem_limit_bytes=VMEM_LIMIT_BYTES),
        name="mixer_step",
    )(x_prompt, x_sample, state_conv[0], jnp.swapaxes(state_pool[0], 0, 1), w_in[0], w_out[0], *small)
    y_prompt, nc_p, np_p, y_sample, nc_s, np_s = outs
    return (y_prompt, y_sample, nc_p, jnp.swapaxes(np_p, 0, 1)[None], nc_s[None],
            jnp.swapaxes(np_s, 0, 1)[None])
```

```python
import functools

import jax
import jax.numpy as jnp
from jax import lax
from jax.experimental import pallas as pl
from jax.experimental.pallas import tpu as pltpu

D_MODEL = 1024
D_CONV = 512
D_POOL = 512
POOL_WINDOWS = (2, 4, 8, 16)
POOL_GC = D_POOL // len(POOL_WINDOWS)
POOL_MAX = max(POOL_WINDOWS)
POOL_HIST = POOL_MAX - 1
CONV_W = 3
CONV_HIST = CONV_W - 1
D_IN = 4 * D_CONV + 2 * D_POOL
EPS = 1e-6

SEQ_TILE = 1024
SUB_TILES = ((0, 384), (384, 384), (768, 256))
BF16_TILE_ROWS = 16
CONV_PAD = 8
POOL_PAD = 16
VMEM_LIMIT_BYTES = 60 * 1024 * 1024

_F32 = jnp.float32

_COL_B, _COL_C, _COL_U, _COL_ZA = 0, D_CONV, 2 * D_CONV, 3 * D_CONV
_COL_V, _COL_ZB = 4 * D_CONV, 4 * D_CONV + D_POOL


def _rmsnorm(x, g):
    ms = jnp.mean(x * x, axis=-1, keepdims=True)
    return (x * lax.rsqrt(ms + EPS)) * g


def _silu(z):
    return z * jax.nn.sigmoid(z)


def _dot(a, b):
    return jnp.dot(a.astype(jnp.bfloat16), b.astype(jnp.bfloat16), preferred_element_type=_F32)


def _pool_project(pooled_groups, pw_ref):
    outs = [_dot(p, pw_ref[gi]) for gi, p in enumerate(pooled_groups)]
    return jnp.concatenate(outs, axis=-1)


def _window_sums(ext):
    assert all(b == 2 * a for a, b in zip((1,) + POOL_WINDOWS, POOL_WINDOWS)) and POOL_PAD >= POOL_MAX
    sums, width, out = ext, 1, []
    for w in POOL_WINDOWS:
        while width < w:
            sums = sums + pltpu.roll(sums, width, axis=0)
            width *= 2
        out.append(sums[POOL_PAD:, 0:POOL_GC])
        sums = sums[:, POOL_GC:]
    return out


def _finish_tile(y_stage, hp_buf, fg_ref):
    r0, n = SUB_TILES[-1]
    y_stage[r0:r0 + n, :] = _rmsnorm(hp_buf[...], fg_ref[...])


def _prompt_tile(b, j, last_j, x_ref, weights, y_cur, y_prev, hp_buf, nc_ref, np_ref, cu_buf, v_buf,
                 weight_copies=None):
    def arrive(k):
        if weight_copies is not None:
            weight_copies[k].wait()

    ng_ref, win_ref, cw_ref, pw_ref, ps_ref, wout_ref, fg_ref = weights
    ts = SEQ_TILE
    head = POOL_MAX

    @pl.when(j == 0)
    def _():
        cu_buf[0:CONV_PAD, :] = jnp.zeros((CONV_PAD, D_CONV), _F32)
        v_buf[0:POOL_PAD, :] = jnp.zeros((POOL_PAD, D_POOL), _F32)

    cw = [cw_ref[k] for k in range(CONV_W)]
    pos = j * ts + 1 + lax.broadcasted_iota(jnp.int32, (head, POOL_GC), 0)

    xs = [x_ref[0, r0:r0 + n, :] for r0, n in SUB_TILES]
    hs = [_rmsnorm(x, ng_ref[...]) for x in xs]
    if weight_copies is None:
        _finish_tile(y_prev, hp_buf, fg_ref)
    arrive(0)
    vs = [_dot(h, win_ref[:, _COL_V:_COL_V + D_POOL]) for h in hs]
    arrive(1)
    zbs = [_dot(h, win_ref[:, _COL_ZB:_COL_ZB + D_POOL]) for h in hs]
    arrive(2)
    projs = [_dot(h, win_ref[:, _COL_B:_COL_ZA + D_CONV]) for h in hs]

    pooled = []
    for s, v in enumerate(vs):
        r0, n = SUB_TILES[s]
        p0 = POOL_PAD + r0
        v_buf[p0:p0 + n, :] = v
        wsums = _window_sums(v_buf[p0 - POOL_PAD:p0 + n, :])
        groups = []
        for gi, w in enumerate(POOL_WINDOWS):
            acc = wsums[gi]
            if s == 0:
                cnt = jnp.minimum(pos, w).astype(_F32)
                mean = jnp.concatenate([acc[0:head] / cnt, acc[head:] * (1.0 / w)], axis=0)
            else:
                mean = acc * (1.0 / w)
            groups.append(mean - v[:, gi * POOL_GC:(gi + 1) * POOL_GC])
        pooled.append(groups)

    y_as = []
    for s, proj in enumerate(projs):
        b_g, c_g, u, z_a = (proj[:, c:c + D_CONV] for c in (_COL_B, _COL_C, _COL_U, _COL_ZA))
        cu = c_g * u
        r0, n = SUB_TILES[s]
        c0 = CONV_PAD + r0
        cu_buf[c0:c0 + n, :] = cu
        yc = cw[0] * cu_buf[c0 - 2:c0 - 2 + n, :]
        yc = yc + cw[1] * cu_buf[c0 - 1:c0 - 1 + n, :]
        yc = yc + cw[2] * cu
        y_as.append(b_g * yc * _silu(z_a))

    y_bs = [_pool_project(p, pw_ref) * ps_ref[...] * _silu(z_b) for p, z_b in zip(pooled, zbs)]
    arrive(3)
    mixes = [_dot(jnp.concatenate([y_a, y_b], axis=-1), wout_ref[...]) for y_a, y_b in zip(y_as, y_bs)]
    for (r0, n), x, mix in zip(SUB_TILES[:-1], xs, mixes):
        y_cur[r0:r0 + n, :] = _rmsnorm(x + mix, fg_ref[...])
    hp_buf[...] = xs[-1] + mixes[-1]

    @pl.when(j == last_j)
    def _():
        nc_ref[0, b] = cu_buf[CONV_PAD + ts - CONV_HIST:CONV_PAD + ts, :]

    tail0 = POOL_PAD + ts - POOL_HIST
    n_seq = np_ref.shape[1]

    @pl.when((j == last_j) & (b == 0))
    def _():
        for k in range(POOL_HIST):
            np_ref[k] = jnp.broadcast_to(v_buf[tail0 + k:tail0 + k + 1, :], (n_seq, D_POOL))

    @pl.when((j == last_j) & (b > 0))
    def _():
        mine = lax.broadcasted_iota(jnp.int32, (n_seq, D_POOL), 0) == b
        for k in range(POOL_HIST):
            row = jnp.broadcast_to(v_buf[tail0 + k:tail0 + k + 1, :], (n_seq, D_POOL))
            np_ref[k] = jnp.where(mine, row, np_ref[k])

    cu_buf[0:CONV_PAD, :] = cu_buf[ts:ts + CONV_PAD, :]
    v_buf[0:POOL_PAD, :] = v_buf[ts:ts + POOL_PAD, :]


def _sample_copies(xs_hbm, sc_hbm, sp_hbm, ys_hbm, ncs_hbm, nps_hbm,
                   xs_buf, sc_buf, sp_buf, ncs_buf, nps_buf, in_sems, out_sems):
    n_tok, n_cs = xs_buf.shape[0], sc_buf.shape[0]
    keep = POOL_HIST - n_tok
    ins = ([pltpu.make_async_copy(xs_hbm.at[:, t, :], xs_buf.at[t], in_sems.at[t]) for t in range(n_tok)]
           + [pltpu.make_async_copy(sc_hbm.at[:, k, :], sc_buf.at[k], in_sems.at[n_tok + k])
              for k in range(n_cs)]
           + [pltpu.make_async_copy(sp_hbm, sp_buf, in_sems.at[n_tok + n_cs]),
              pltpu.make_async_copy(sp_hbm.at[pl.ds(n_tok, keep)], nps_hbm.at[pl.ds(0, keep)],
                                    in_sems.at[n_tok + n_cs + 1])])
    outs = ([pltpu.make_async_copy(xs_buf.at[t], ys_hbm.at[:, t, :], out_sems.at[t]) for t in range(n_tok)]
            + [pltpu.make_async_copy(ncs_buf.at[k], ncs_hbm.at[:, k, :], out_sems.at[n_tok + k])
               for k in range(n_cs)]
            + [pltpu.make_async_copy(nps_buf, nps_hbm.at[pl.ds(keep, n_tok)], out_sems.at[n_tok + n_cs])])
    return ins, outs


def _sample_step(weights, xs_buf, sc_buf, sp_buf, ncs_buf, nps_buf, y_prev, hp_buf):
    ng_ref, win_ref, cw_ref, pw_ref, ps_ref, wout_ref, fg_ref = weights
    n_tok, n_seq, _ = xs_buf.shape

    def tok(a, t):
        return a[t * n_seq:(t + 1) * n_seq, :]

    x = jnp.concatenate([xs_buf[t] for t in range(n_tok)], axis=0)
    h = _rmsnorm(x, ng_ref[...])
    _finish_tile(y_prev, hp_buf, fg_ref)
    v = _dot(h, win_ref[:, _COL_V:_COL_V + D_POOL])
    z_b = _dot(h, win_ref[:, _COL_ZB:_COL_ZB + D_POOL])
    proj = _dot(h, win_ref[:, _COL_B:_COL_ZA + D_CONV])

    vp = [sp_buf[k] for k in range(POOL_HIST)] + [tok(v, t) for t in range(n_tok)]
    for t in range(n_tok):
        nps_buf[t] = tok(v, t)
    pooled = []
    for gi, w in enumerate(POOL_WINDOWS):
        lo = gi * POOL_GC
        rows = []
        for t in range(n_tok):
            vg = vp[POOL_HIST + t][:, lo:lo + POOL_GC]
            acc = vg
            for k in range(1, w):
                acc = acc + vp[POOL_HIST + t - k][:, lo:lo + POOL_GC]
            rows.append(acc * (1.0 / w) - vg)
        pooled.append(jnp.concatenate(rows, axis=0))
    y_b = _pool_project(pooled, pw_ref) * ps_ref[...] * _silu(z_b)

    b_g, c_g, u, z_a = (proj[:, c:c + D_CONV] for c in (_COL_B, _COL_C, _COL_U, _COL_ZA))
    cu = c_g * u
    cw = [cw_ref[k] for k in range(CONV_W)]
    up = [sc_buf[k] for k in range(CONV_HIST)] + [tok(cu, t) for t in range(n_tok)]
    for k in range(CONV_HIST):
        ncs_buf[k] = up[n_tok + k]
    yc = []
    for t in range(n_tok):
        acc = cw[0] * up[t]
        for k in range(1, CONV_W):
            acc = acc + cw[k] * up[t + k]
        yc.append(acc)
    y_a = b_g * jnp.concatenate(yc, axis=0) * _silu(z_a)

    ycat = jnp.concatenate([y_a, y_b], axis=-1)
    y = _rmsnorm(x + _dot(ycat, wout_ref[...]), fg_ref[...])
    for t in range(n_tok):
        xs_buf[t] = tok(y, t)


def _mixer_kernel(x_ref, xs_hbm, sc_hbm, sp_hbm, win_hbm, wout_hbm, ng_ref, cw_ref, pw_ref, ps_ref, fg_ref,
                  y_hbm, nc_ref, np_ref, ys_hbm, ncs_hbm, nps_hbm,
                  win_ref, wout_ref, y_stage, hp_buf, cu_buf, v_buf, xs_buf, sc_buf, sp_buf, ncs_buf, nps_buf,
                  w_sems, y_sems, in_sems, out_sems, *, tiles_per_seq):
    i = pl.program_id(0)
    n_prompt = pl.num_programs(0) - 1
    weights = (ng_ref, win_ref, cw_ref, pw_ref, ps_ref, wout_ref, fg_ref)
    w_cols = ((_COL_V, D_POOL), (_COL_ZB, D_POOL), (_COL_B, 4 * D_CONV))
    weight_copies = [pltpu.make_async_copy(win_hbm.at[:, pl.ds(c, n)], win_ref.at[:, pl.ds(c, n)],
                                           w_sems.at[k]) for k, (c, n) in enumerate(w_cols)]
    weight_copies.append(pltpu.make_async_copy(wout_hbm, wout_ref, w_sems.at[len(w_cols)]))
    ins, outs = _sample_copies(xs_hbm, sc_hbm, sp_hbm, ys_hbm, ncs_hbm, nps_hbm,
                               xs_buf, sc_buf, sp_buf, ncs_buf, nps_buf, in_sems, out_sems)

    tail_r0, tail_n = SUB_TILES[-1]

    def y_copy(p, tail):
        slot = p % 2
        r0, n = (tail_r0, tail_n) if tail else (0, tail_r0)
        return pltpu.make_async_copy(
            y_stage.at[slot, pl.ds(r0, n)],
            y_hbm.at[p // tiles_per_seq, pl.ds((p % tiles_per_seq) * SEQ_TILE + r0, n)],
            y_sems.at[2 * slot + int(tail)])

    @pl.when(i >= 2)
    def _():
        y_copy(i - 2, False).wait()

    @pl.when(i >= 3)
    def _():
        y_copy(i - 3, True).wait()

    y_cur, y_prev = y_stage.at[i % 2], y_stage.at[(i + 1) % 2]

    @pl.when(i == 0)
    def _():
        for cp in weight_copies + ins:
            cp.start()
        _prompt_tile(0, 0, tiles_per_seq - 1, x_ref, weights, y_cur, y_prev, hp_buf, nc_ref, np_ref,
                     cu_buf, v_buf, weight_copies)
        y_copy(i, False).start()

    @pl.when((i > 0) & (i < n_prompt))
    def _():
        _prompt_tile(i // tiles_per_seq, i % tiles_per_seq, tiles_per_seq - 1,
                     x_ref, weights, y_cur, y_prev, hp_buf, nc_ref, np_ref, cu_buf, v_buf)
        y_copy(i, False).start()
        y_copy(i - 1, True).start()

    @pl.when(i == n_prompt)
    def _():
        for cp in ins:
            cp.wait()
        _sample_step(weights, xs_buf, sc_buf, sp_buf, ncs_buf, nps_buf, y_prev, hp_buf)
        y_copy(i - 1, True).start()
        for cp in outs:
            cp.start()
        for cp in outs:
            cp.wait()
        y_copy(i - 1, False).wait()
        y_copy(i - 2, True).wait()
        y_copy(i - 1, True).wait()


def _const_spec(shape):
    nd = len(shape)
    return pl.BlockSpec(shape, lambda *_: (0,) * nd, pipeline_mode=pl.Buffered(1))


def kernel(x_prompt, x_sample, state_conv, state_pool, norm_g, w_in, conv_w, pool_w, pool_scale,
           w_out, final_g):
    assert norm_g.shape[0] == 1, "single-layer trunk"
    batch, seq, _ = x_prompt.shape
    n_seq, n_tok, _ = x_sample.shape
    assert seq % SEQ_TILE == 0 and n_tok < POOL_HIST
    assert SUB_TILES[0][0] == 0 and sum(n for _, n in SUB_TILES) == SEQ_TILE
    assert all(a + n == b for (a, n), (b, _) in zip(SUB_TILES, SUB_TILES[1:]))
    assert all(n % BF16_TILE_ROWS == 0 and n >= POOL_MAX for _, n in SUB_TILES)
    tiles_per_seq = seq // SEQ_TILE
    n_prompt = batch * tiles_per_seq

    small = (norm_g, jnp.swapaxes(conv_w, 0, 1), pool_w[0], pool_scale, final_g.reshape(1, D_MODEL))

    def tile(i):
        p = jnp.minimum(i, n_prompt - 1)
        return p // tiles_per_seq, p % tiles_per_seq

    hbm = pl.BlockSpec(memory_space=pl.ANY)
    outs = pl.pallas_call(
        functools.partial(_mixer_kernel, tiles_per_seq=tiles_per_seq),
        grid=(n_prompt + 1,),
        in_specs=[pl.BlockSpec((1, SEQ_TILE, D_MODEL), lambda i: (*tile(i), 0))] + [hbm] * 5
        + [_const_spec(a.shape) for a in small],
        out_specs=[
            hbm,
            pl.BlockSpec((1, batch, CONV_HIST, D_CONV), lambda i: (0, 0, 0, 0)),
            pl.BlockSpec((POOL_HIST, batch, D_POOL), lambda i: (0, 0, 0)),
            hbm, hbm, hbm,
        ],
        out_shape=[
            jax.ShapeDtypeStruct((batch, seq, D_MODEL), _F32),
            jax.ShapeDtypeStruct((1, batch, CONV_HIST, D_CONV), _F32),
            jax.ShapeDtypeStruct((POOL_HIST, batch, D_POOL), _F32),
            jax.ShapeDtypeStruct((n_seq, n_tok, D_MODEL), _F32),
            jax.ShapeDtypeStruct((n_seq, CONV_HIST, D_CONV), _F32),
            jax.ShapeDtypeStruct((POOL_HIST, n_seq, D_POOL), _F32),
        ],
        scratch_shapes=[
            pltpu.VMEM((D_MODEL, D_IN), _F32),
            pltpu.VMEM((D_MODEL, D_MODEL), _F32),
            pltpu.VMEM((2, SEQ_TILE, D_MODEL), _F32),
            pltpu.VMEM((SUB_TILES[-1][1], D_MODEL), _F32),
            pltpu.VMEM((CONV_PAD + SEQ_TILE, D_CONV), _F32),
            pltpu.VMEM((POOL_PAD + SEQ_TILE, D_POOL), _F32),
            pltpu.VMEM((n_tok, n_seq, D_MODEL), _F32),
            pltpu.VMEM((CONV_HIST, n_seq, D_CONV), _F32),
            pltpu.VMEM((POOL_HIST, n_seq, D_POOL), _F32),
            pltpu.VMEM((CONV_HIST, n_seq, D_CONV), _F32),
            pltpu.VMEM((n_tok, n_seq, D_POOL), _F32),
            pltpu.SemaphoreType.DMA((4,)),
            pltpu.SemaphoreType.DMA((4,)),
            pltpu.SemaphoreType.DMA((n_tok + CONV_HIST + 2,)),
            pltpu.SemaphoreType.DMA((n_tok + CONV_HIST + 1,)),
        ],
        compiler_params=pltpu.CompilerParams(
            dimension_semantics=("arbitrary",),
            vmem_limit_bytes=VMEM_LIMIT_BYTES),
        name="mixer_step",
    )(x_prompt, x_sample, state_conv[0], jnp.swapaxes(state_pool[0], 0, 1), w_in[0], w_out[0], *small)
    y_prompt, nc_p, np_p, y_sample, nc_s, np_s = outs
    return (y_prompt, y_sample, nc_p, jnp.swapaxes(np_p, 0, 1)[None], nc_s[None],
            jnp.swapaxes(np_s, 0, 1)[None])
```

```python
import functools

import jax
import jax.numpy as jnp
from jax import lax
from jax.experimental import pallas as pl
from jax.experimental.pallas import tpu as pltpu

D_MODEL = 1024
D_CONV = 512
D_POOL = 512
POOL_WINDOWS = (2, 4, 8, 16)
POOL_GC = D_POOL // len(POOL_WINDOWS)
POOL_MAX = max(POOL_WINDOWS)
POOL_HIST = POOL_MAX - 1
CONV_W = 3
CONV_HIST = CONV_W - 1
D_IN = 4 * D_CONV + 2 * D_POOL
EPS = 1e-6

SEQ_TILE = 1024
SUB_TILES = ((0, 384), (384, 384), (768, 256))
BF16_TILE_ROWS = 16
CONV_PAD = 8
POOL_PAD = 16
VMEM_LIMIT_BYTES = 60 * 1024 * 1024

_F32 = jnp.float32

_COL_B, _COL_C, _COL_U, _COL_ZA = 0, D_CONV, 2 * D_CONV, 3 * D_CONV
_COL_V, _COL_ZB = 4 * D_CONV, 4 * D_CONV + D_POOL


def _rmsnorm(x, g):
    ms = jnp.mean(x * x, axis=-1, keepdims=True)
    return (x * lax.rsqrt(ms + EPS)) * g


def _silu(z):
    return z * jax.nn.sigmoid(z)


def _dot(a, b):
    return jnp.dot(a.astype(jnp.bfloat16), b.astype(jnp.bfloat16), preferred_element_type=_F32)


def _pool_project(pooled_groups, pw_ref):
    outs = [_dot(p, pw_ref[gi]) for gi, p in enumerate(pooled_groups)]
    return jnp.concatenate(outs, axis=-1)


def _window_sums(ext):
    assert all(b == 2 * a for a, b in zip((1,) + POOL_WINDOWS, POOL_WINDOWS)) and POOL_PAD >= POOL_MAX
    sums, width, out = ext, 1, []
    for w in POOL_WINDOWS:
        while width < w:
            sums = sums + pltpu.roll(sums, width, axis=0)
            width *= 2
        out.append(sums[POOL_PAD:, 0:POOL_GC])
        sums = sums[:, POOL_GC:]
    return out


def _finish_tile(y_stage, hp_buf, fg_ref):
    r0, n = SUB_TILES[-1]
    y_stage[r0:r0 + n, :] = _rmsnorm(hp_buf[...], fg_ref[...])


def _prompt_tile(b, j, last_j, x_ref, weights, y_cur, y_prev, hp_buf, nc_ref, np_ref, cu_buf, v_buf,
                 weight_copies=None):
    def arrive(k):
        if weight_copies is not None:
            weight_copies[k].wait()

    ng_ref, win_ref, cw_ref, pw_ref, ps_ref, wout_ref, fg_ref = weights
    ts = SEQ_TILE
    head = POOL_MAX

    @pl.when(j == 0)
    def _():
        cu_buf[0:CONV_PAD, :] = jnp.zeros((CONV_PAD, D_CONV), _F32)
        v_buf[0:POOL_PAD, :] = jnp.zeros((POOL_PAD, D_POOL), _F32)

    cw = [cw_ref[k] for k in range(CONV_W)]
    pos = j * ts + 1 + lax.broadcasted_iota(jnp.int32, (head, POOL_GC), 0)

    xs = [x_ref[0, r0:r0 + n, :] for r0, n in SUB_TILES]
    hs = [_rmsnorm(x, ng_ref[...]) for x in xs]
    if weight_copies is None:
        _finish_tile(y_prev, hp_buf, fg_ref)
    arrive(0)
    vs = [_dot(h, win_ref[:, _COL_V:_COL_V + D_POOL]) for h in hs]
    arrive(1)
    zbs = [_dot(h, win_ref[:, _COL_ZB:_COL_ZB + D_POOL]) for h in hs]
    arrive(2)
    projs = [_dot(h, win_ref[:, _COL_B:_COL_ZA + D_CONV]) for h in hs]

    pooled = []
    for s, v in enumerate(vs):
        r0, n = SUB_TILES[s]
        p0 = POOL_PAD + r0
        v_buf[p0:p0 + n, :] = v
        wsums = _window_sums(v_buf[p0 - POOL_PAD:p0 + n, :])
        groups = []
        for gi, w in enumerate(POOL_WINDOWS):
            acc = wsums[gi]
            if s == 0:
                cnt = jnp.minimum(pos, w).astype(_F32)
                mean = jnp.concatenate([acc[0:head] / cnt, acc[head:] * (1.0 / w)], axis=0)
            else:
                mean = acc * (1.0 / w)
            groups.append(mean - v[:, gi * POOL_GC:(gi + 1) * POOL_GC])
        pooled.append(groups)

    y_as = []
    for s, proj in enumerate(projs):
        b_g, c_g, u, z_a = (proj[:, c:c + D_CONV] for c in (_COL_B, _COL_C, _COL_U, _COL_ZA))
        cu = c_g * u
        r0, n = SUB_TILES[s]
        c0 = CONV_PAD + r0
        cu_buf[c0:c0 + n, :] = cu
        yc = cw[0] * cu_buf[c0 - 2:c0 - 2 + n, :]
        yc = yc + cw[1] * cu_buf[c0 - 1:c0 - 1 + n, :]
        yc = yc + cw[2] * cu
        y_as.append(b_g * yc * _silu(z_a))

    y_bs = [_pool_project(p, pw_ref) * ps_ref[...] * _silu(z_b) for p, z_b in zip(pooled, zbs)]
    arrive(3)
    mixes = [_dot(jnp.concatenate([y_a, y_b], axis=-1), wout_ref[...]) for y_a, y_b in zip(y_as, y_bs)]
    for (r0, n), x, mix in zip(SUB_TILES[:-1], xs, mixes):
        y_cur[r0:r0 + n, :] = _rmsnorm(x + mix, fg_ref[...])
    hp_buf[...] = xs[-1] + mixes[-1]

    @pl.when(j == last_j)
    def _():
        nc_ref[0, b] = cu_buf[CONV_PAD + ts - CONV_HIST:CONV_PAD + ts, :]

    tail0 = POOL_PAD + ts - POOL_HIST
    n_seq = np_ref.shape[1]

    @pl.when((j == last_j) & (b == 0))
    def _():
        for k in range(POOL_HIST):
            np_ref[k] = jnp.broadcast_to(v_buf[tail0 + k:tail0 + k + 1, :], (n_seq, D_POOL))

    @pl.when((j == last_j) & (b > 0))
    def _():
        mine = lax.broadcasted_iota(jnp.int32, (n_seq, D_POOL), 0) == b
        for k in range(POOL_HIST):
            row = jnp.broadcast_to(v_buf[tail0 + k:tail0 + k + 1, :], (n_seq, D_POOL))
            np_ref[k] = jnp.where(mine, row, np_ref[k])

    cu_buf[0:CONV_PAD, :] = cu_buf[ts:ts + CONV_PAD, :]
    v_buf[0:POOL_PAD, :] = v_buf[ts:ts + POOL_PAD, :]


def _sample_copies(xs_hbm, sc_hbm, sp_hbm, ys_hbm, ncs_hbm, nps_hbm,
                   xs_buf, sc_buf, sp_buf, ncs_buf, nps_buf, in_sems, out_sems):
    n_tok, n_cs = xs_buf.shape[0], sc_buf.shape[0]
    keep = POOL_HIST - n_tok
    ins = ([pltpu.make_async_copy(xs_hbm.at[:, t, :], xs_buf.at[t], in_sems.at[t]) for t in range(n_tok)]
           + [pltpu.make_async_copy(sc_hbm.at[:, k, :], sc_buf.at[k], in_sems.at[n_tok + k])
              for k in range(n_cs)]
           + [pltpu.make_async_copy(sp_hbm, sp_buf, in_sems.at[n_tok + n_cs]),
              pltpu.make_async_copy(sp_hbm.at[pl.ds(n_tok, keep)], nps_hbm.at[pl.ds(0, keep)],
                                    in_sems.at[n_tok + n_cs + 1])])
    outs = ([pltpu.make_async_copy(xs_buf.at[t], ys_hbm.at[:, t, :], out_sems.at[t]) for t in range(n_tok)]
            + [pltpu.make_async_copy(ncs_buf.at[k], ncs_hbm.at[:, k, :], out_sems.at[n_tok + k])
               for k in range(n_cs)]
            + [pltpu.make_async_copy(nps_buf, nps_hbm.at[pl.ds(keep, n_tok)], out_sems.at[n_tok + n_cs])])
    return ins, outs


def _sample_step(weights, xs_buf, sc_buf, sp_buf, ncs_buf, nps_buf, y_prev, hp_buf):
    ng_ref, win_ref, cw_ref, pw_ref, ps_ref, wout_ref, fg_ref = weights
    n_tok, n_seq, _ = xs_buf.shape

    def tok(a, t):
        return a[t * n_seq:(t + 1) * n_seq, :]

    x = jnp.concatenate([xs_buf[t] for t in range(n_tok)], axis=0)
    h = _rmsnorm(x, ng_ref[...])
    _finish_tile(y_prev, hp_buf, fg_ref)
    v = _dot(h, win_ref[:, _COL_V:_COL_V + D_POOL])
    z_b = _dot(h, win_ref[:, _COL_ZB:_COL_ZB + D_POOL])
    proj = _dot(h, win_ref[:, _COL_B:_COL_ZA + D_CONV])

    vp = [sp_buf[k] for k in range(POOL_HIST)] + [tok(v, t) for t in range(n_tok)]
    for t in range(n_tok):
        nps_buf[t] = tok(v, t)
    pooled = []
    for gi, w in enumerate(POOL_WINDOWS):
        lo = gi * POOL_GC
        rows = []
        for t in range(n_tok):
            vg = vp[POOL_HIST + t][:, lo:lo + POOL_GC]
            acc = vg
            for k in range(1, w):
                acc = acc + vp[POOL_HIST + t - k][:, lo:lo + POOL_GC]
            rows.append(acc * (1.0 / w) - vg)
        pooled.append(jnp.concatenate(rows, axis=0))
    y_b = _pool_project(pooled, pw_ref) * ps_ref[...] * _silu(z_b)

    b_g, c_g, u, z_a = (proj[:, c:c + D_CONV] for c in (_COL_B, _COL_C, _COL_U, _COL_ZA))
    cu = c_g * u
    cw = [cw_ref[k] for k in range(CONV_W)]
    up = [sc_buf[k] for k in range(CONV_HIST)] + [tok(cu, t) for t in range(n_tok)]
    for k in range(CONV_HIST):
        ncs_buf[k] = up[n_tok + k]
    yc = []
    for t in range(n_tok):
        acc = cw[0] * up[t]
        for k in range(1, CONV_W):
            acc = acc + cw[k] * up[t + k]
        yc.append(acc)
    y_a = b_g * jnp.concatenate(yc, axis=0) * _silu(z_a)

    ycat = jnp.concatenate([y_a, y_b], axis=-1)
    y = _rmsnorm(x + _dot(ycat, wout_ref[...]), fg_ref[...])
    for t in range(n_tok):
        xs_buf[t] = tok(y, t)


def _mixer_kernel(x_ref, xs_hbm, sc_hbm, sp_hbm, win_hbm, wout_hbm, ng_ref, cw_ref, pw_ref, ps_ref, fg_ref,
                  y_hbm, nc_ref, np_ref, ys_hbm, ncs_hbm, nps_hbm,
                  win_ref, wout_ref, y_stage, hp_buf, cu_buf, v_buf, xs_buf, sc_buf, sp_buf, ncs_buf, nps_buf,
                  w_sems, y_sems, in_sems, out_sems, *, tiles_per_seq):
    i = pl.program_id(0)
    n_prompt = pl.num_programs(0) - 1
    weights = (ng_ref, win_ref, cw_ref, pw_ref, ps_ref, wout_ref, fg_ref)
    w_cols = ((_COL_V, D_POOL), (_COL_ZB, D_POOL), (_COL_B, 4 * D_CONV))
    weight_copies = [pltpu.make_async_copy(win_hbm.at[:, pl.ds(c, n)], win_ref.at[:, pl.ds(c, n)],
                                           w_sems.at[k]) for k, (c, n) in enumerate(w_cols)]
    weight_copies.append(pltpu.make_async_copy(wout_hbm, wout_ref, w_sems.at[len(w_cols)]))
    ins, outs = _sample_copies(xs_hbm, sc_hbm, sp_hbm, ys_hbm, ncs_hbm, nps_hbm,
                               xs_buf, sc_buf, sp_buf, ncs_buf, nps_buf, in_sems, out_sems)

    tail_r0, tail_n = SUB_TILES[-1]

    def y_copy(p, tail):
        slot = p % 2
        r0, n = (tail_r0, tail_n) if tail else (0, tail_r0)
        return pltpu.make_async_copy(
            y_stage.at[slot, pl.ds(r0, n)],
            y_hbm.at[p // tiles_per_seq, pl.ds((p % tiles_per_seq) * SEQ_TILE + r0, n)],
            y_sems.at[2 * slot + int(tail)])

    @pl.when(i >= 2)
    def _():
        y_copy(i - 2, False).wait()

    @pl.when(i >= 3)
    def _():
        y_copy(i - 3, True).wait()

    y_cur, y_prev = y_stage.at[i % 2], y_stage.at[(i + 1) % 2]

    @pl.when(i == 0)
    def _():
        for cp in weight_copies + ins:
            cp.start()
        _prompt_tile(0, 0, tiles_per_seq - 1, x_ref, weights, y_cur, y_prev, hp_buf, nc_ref, np_ref,
                     cu_buf, v_buf, weight_copies)
        y_copy(i, False).start()

    @pl.when((i > 0) & (i < n_prompt))
    def _():
        _prompt_tile(i // tiles_per_seq, i % tiles_per_seq, tiles_per_seq - 1,
                     x_ref, weights, y_cur, y_prev, hp_buf, nc_ref, np_ref, cu_buf, v_buf)
        y_copy(i, False).start()
        y_copy(i - 1, True).start()

    @pl.when(i == n_prompt)
    def _():
        for cp in ins:
            cp.wait()
        _sample_step(weights, xs_buf, sc_buf, sp_buf, ncs_buf, nps_buf, y_prev, hp_buf)
        y_copy(i - 1, True).start()
        for cp in outs:
            cp.start()
        for cp in outs:
            cp.wait()
        y_copy(i - 1, False).wait()
        y_copy(i - 2, True).wait()
        y_copy(i - 1, True).wait()


def _const_spec(shape):
    nd = len(shape)
    return pl.BlockSpec(shape, lambda *_: (0,) * nd, pipeline_mode=pl.Buffered(1))


def kernel(x_prompt, x_sample, state_conv, state_pool, norm_g, w_in, conv_w, pool_w, pool_scale,
           w_out, final_g):
    assert norm_g.shape[0] == 1, "single-layer trunk"
    batch, seq, _ = x_prompt.shape
    n_seq, n_tok, _ = x_sample.shape
    assert seq % SEQ_TILE == 0 and n_tok < POOL_HIST
    assert SUB_TILES[0][0] == 0 and sum(n for _, n in SUB_TILES) == SEQ_TILE
    assert all(a + n == b for (a, n), (b, _) in zip(SUB_TILES, SUB_TILES[1:]))
    assert all(n % BF16_TILE_ROWS == 0 and n >= POOL_MAX for _, n in SUB_TILES)
    tiles_per_seq = seq // SEQ_TILE
    n_prompt = batch * tiles_per_seq
    assert n_prompt >= 2, "the output copy schedule drains two tiles in the decode step"

    small = (norm_g, jnp.swapaxes(conv_w, 0, 1), pool_w[0], pool_scale, final_g.reshape(1, D_MODEL))

    def tile(i):
        p = jnp.minimum(i, n_prompt - 1)
        return p // tiles_per_seq, p % tiles_per_seq

    hbm = pl.BlockSpec(memory_space=pl.ANY)
    outs = pl.pallas_call(
        functools.partial(_mixer_kernel, tiles_per_seq=tiles_per_seq),
        grid=(n_prompt + 1,),
        in_specs=[pl.BlockSpec((1, SEQ_TILE, D_MODEL), lambda i: (*tile(i), 0))] + [hbm] * 5
        + [_const_spec(a.shape) for a in small],
        out_specs=[
            hbm,
            pl.BlockSpec((1, batch, CONV_HIST, D_CONV), lambda i: (0, 0, 0, 0)),
            pl.BlockSpec((POOL_HIST, batch, D_POOL), lambda i: (0, 0, 0)),
            hbm, hbm, hbm,
        ],
        out_shape=[
            jax.ShapeDtypeStruct((batch, seq, D_MODEL), _F32),
            jax.ShapeDtypeStruct((1, batch, CONV_HIST, D_CONV), _F32),
            jax.ShapeDtypeStruct((POOL_HIST, batch, D_POOL), _F32),
            jax.ShapeDtypeStruct((n_seq, n_tok, D_MODEL), _F32),
            jax.ShapeDtypeStruct((n_seq, CONV_HIST, D_CONV), _F32),
            jax.ShapeDtypeStruct((POOL_HIST, n_seq, D_POOL), _F32),
        ],
        scratch_shapes=[
            pltpu.VMEM((D_MODEL, D_IN), _F32),
            pltpu.VMEM((D_MODEL, D_MODEL), _F32),
            pltpu.VMEM((2, SEQ_TILE, D_MODEL), _F32),
            pltpu.VMEM((SUB_TILES[-1][1], D_MODEL), _F32),
            pltpu.VMEM((CONV_PAD + SEQ_TILE, D_CONV), _F32),
            pltpu.VMEM((POOL_PAD + SEQ_TILE, D_POOL), _F32),
            pltpu.VMEM((n_tok, n_seq, D_MODEL), _F32),
            pltpu.VMEM((CONV_HIST, n_seq, D_CONV), _F32),
            pltpu.VMEM((POOL_HIST, n_seq, D_POOL), _F32),
            pltpu.VMEM((CONV_HIST, n_seq, D_CONV), _F32),
            pltpu.VMEM((n_tok, n_seq, D_POOL), _F32),
            pltpu.SemaphoreType.DMA((4,)),
            pltpu.SemaphoreType.DMA((4,)),
            pltpu.SemaphoreType.DMA((n_tok + CONV_HIST + 2,)),
            pltpu.SemaphoreType.DMA((n_tok + CONV_HIST + 1,)),
        ],
        compiler_params=pltpu.CompilerParams(
            dimension_semantics=("arbitrary",),
            vmem_limit_bytes=VMEM_LIMIT_BYTES),
        name="mixer_step",
    )(x_prompt, x_sample, state_conv[0], jnp.swapaxes(state_pool[0], 0, 1), w_in[0], w_out[0], *small)
    y_prompt, nc_p, np_p, y_sample, nc_s, np_s = outs
    return (y_prompt, y_sample, nc_p, jnp.swapaxes(np_p, 0, 1)[None], nc_s[None],
            jnp.swapaxes(np_s, 0, 1)[None])
```

```python
import functools

import jax
import jax.numpy as jnp
from jax import lax
from jax.experimental import pallas as pl
from jax.experimental.pallas import tpu as pltpu

D_MODEL = 1024
D_CONV = 512
D_POOL = 512
POOL_WINDOWS = (2, 4, 8, 16)
POOL_GC = D_POOL // len(POOL_WINDOWS)
POOL_MAX = max(POOL_WINDOWS)
POOL_HIST = POOL_MAX - 1
CONV_W = 3
CONV_HIST = CONV_W - 1
D_IN = 4 * D_CONV + 2 * D_POOL
EPS = 1e-6

SEQ_TILE = 1024
SUB_TILES = ((0, 384), (384, 384), (768, 256))
BF16_TILE_ROWS = 16
CONV_PAD = 8
POOL_PAD = 16
VMEM_LIMIT_BYTES = 60 * 1024 * 1024

_F32 = jnp.float32

_COL_B, _COL_C, _COL_U, _COL_ZA = 0, D_CONV, 2 * D_CONV, 3 * D_CONV
_COL_V, _COL_ZB = 4 * D_CONV, 4 * D_CONV + D_POOL


def _rmsnorm(x, g):
    ms = jnp.mean(x * x, axis=-1, keepdims=True)
    return (x * lax.rsqrt(ms + EPS)) * g


def _silu(z):
    return z * jax.nn.sigmoid(z)


def _dot(a, b):
    return jnp.dot(a.astype(jnp.bfloat16), b.astype(jnp.bfloat16), preferred_element_type=_F32)


def _pool_project(pooled_groups, pw_ref):
    outs = [_dot(p, pw_ref[gi]) for gi, p in enumerate(pooled_groups)]
    return jnp.concatenate(outs, axis=-1)


def _window_sums(ext):
    assert all(b == 2 * a for a, b in zip((1,) + POOL_WINDOWS, POOL_WINDOWS)) and POOL_PAD >= POOL_MAX
    sums, width, out = ext, 1, []
    for w in POOL_WINDOWS:
        while width < w:
            sums = sums + pltpu.roll(sums, width, axis=0)
            width *= 2
        out.append(sums[POOL_PAD:, 0:POOL_GC])
        sums = sums[:, POOL_GC:]
    return out


def _finish_tile(y_stage, hp_buf, fg_ref):
    r0, n = SUB_TILES[-1]
    y_stage[r0:r0 + n, :] = _rmsnorm(hp_buf[...], fg_ref[...])


def _prompt_tile(b, j, last_j, x_ref, weights, y_cur, y_prev, hp_buf, nc_ref, np_ref, cu_buf, v_buf,
                 arrive=None):
    first = arrive is not None
    if not first:
        def arrive(k):
            pass

    ng_ref, win_ref, cw_ref, pw_ref, ps_ref, wout_ref, fg_ref = weights
    ts = SEQ_TILE
    head = POOL_MAX

    @pl.when(j == 0)
    def _():
        cu_buf[0:CONV_PAD, :] = jnp.zeros((CONV_PAD, D_CONV), _F32)
        v_buf[0:POOL_PAD, :] = jnp.zeros((POOL_PAD, D_POOL), _F32)

    cw = [cw_ref[k] for k in range(CONV_W)]
    pos = j * ts + 1 + lax.broadcasted_iota(jnp.int32, (head, POOL_GC), 0)

    xs = [x_ref[0, r0:r0 + n, :] for r0, n in SUB_TILES]
    hs = [_rmsnorm(x, ng_ref[...]) for x in xs]
    if not first:
        _finish_tile(y_prev, hp_buf, fg_ref)
    arrive(0)
    vs = [_dot(h, win_ref[:, _COL_V:_COL_V + D_POOL]) for h in hs]
    arrive(1)
    zbs = [_dot(h, win_ref[:, _COL_ZB:_COL_ZB + D_POOL]) for h in hs]
    arrive(2)
    projs = [_dot(h, win_ref[:, _COL_B:_COL_ZA + D_CONV]) for h in hs]

    pooled = []
    for s, v in enumerate(vs):
        r0, n = SUB_TILES[s]
        p0 = POOL_PAD + r0
        v_buf[p0:p0 + n, :] = v
        wsums = _window_sums(v_buf[p0 - POOL_PAD:p0 + n, :])
        groups = []
        for gi, w in enumerate(POOL_WINDOWS):
            acc = wsums[gi]
            if s == 0:
                cnt = jnp.minimum(pos, w).astype(_F32)
                mean = jnp.concatenate([acc[0:head] / cnt, acc[head:] * (1.0 / w)], axis=0)
            else:
                mean = acc * (1.0 / w)
            groups.append(mean - v[:, gi * POOL_GC:(gi + 1) * POOL_GC])
        pooled.append(groups)

    y_as = []
    for s, proj in enumerate(projs):
        b_g, c_g, u, z_a = (proj[:, c:c + D_CONV] for c in (_COL_B, _COL_C, _COL_U, _COL_ZA))
        cu = c_g * u
        r0, n = SUB_TILES[s]
        c0 = CONV_PAD + r0
        cu_buf[c0:c0 + n, :] = cu
        yc = cw[0] * cu_buf[c0 - 2:c0 - 2 + n, :]
        yc = yc + cw[1] * cu_buf[c0 - 1:c0 - 1 + n, :]
        yc = yc + cw[2] * cu
        y_as.append(b_g * yc * _silu(z_a))

    y_bs = [_pool_project(p, pw_ref) * ps_ref[...] * _silu(z_b) for p, z_b in zip(pooled, zbs)]
    arrive(3)
    mixes = [_dot(jnp.concatenate([y_a, y_b], axis=-1), wout_ref[...]) for y_a, y_b in zip(y_as, y_bs)]
    for (r0, n), x, mix in zip(SUB_TILES[:-1], xs, mixes):
        y_cur[r0:r0 + n, :] = _rmsnorm(x + mix, fg_ref[...])
    hp_buf[...] = xs[-1] + mixes[-1]

    @pl.when(j == last_j)
    def _():
        nc_ref[0, b] = cu_buf[CONV_PAD + ts - CONV_HIST:CONV_PAD + ts, :]

    tail0 = POOL_PAD + ts - POOL_HIST
    n_seq = np_ref.shape[1]

    @pl.when((j == last_j) & (b == 0))
    def _():
        for k in range(POOL_HIST):
            np_ref[k] = jnp.broadcast_to(v_buf[tail0 + k:tail0 + k + 1, :], (n_seq, D_POOL))

    @pl.when((j == last_j) & (b > 0))
    def _():
        mine = lax.broadcasted_iota(jnp.int32, (n_seq, D_POOL), 0) == b
        for k in range(POOL_HIST):
            row = jnp.broadcast_to(v_buf[tail0 + k:tail0 + k + 1, :], (n_seq, D_POOL))
            np_ref[k] = jnp.where(mine, row, np_ref[k])

    cu_buf[0:CONV_PAD, :] = cu_buf[ts:ts + CONV_PAD, :]
    v_buf[0:POOL_PAD, :] = v_buf[ts:ts + POOL_PAD, :]


def _sample_copies(xs_hbm, sc_hbm, sp_hbm, ys_hbm, ncs_hbm, nps_hbm,
                   xs_buf, sc_buf, sp_buf, ncs_buf, nps_buf, in_sems, out_sems):
    n_tok, n_cs = xs_buf.shape[0], sc_buf.shape[0]
    keep = POOL_HIST - n_tok
    ins = ([pltpu.make_async_copy(xs_hbm.at[:, t, :], xs_buf.at[t], in_sems.at[t]) for t in range(n_tok)]
           + [pltpu.make_async_copy(sc_hbm.at[:, k, :], sc_buf.at[k], in_sems.at[n_tok + k])
              for k in range(n_cs)]
           + [pltpu.make_async_copy(sp_hbm, sp_buf, in_sems.at[n_tok + n_cs]),
              pltpu.make_async_copy(sp_hbm.at[pl.ds(n_tok, keep)], nps_hbm.at[pl.ds(0, keep)],
                                    in_sems.at[n_tok + n_cs + 1])])
    outs = ([pltpu.make_async_copy(xs_buf.at[t], ys_hbm.at[:, t, :], out_sems.at[t]) for t in range(n_tok)]
            + [pltpu.make_async_copy(ncs_buf.at[k], ncs_hbm.at[:, k, :], out_sems.at[n_tok + k])
               for k in range(n_cs)]
            + [pltpu.make_async_copy(nps_buf, nps_hbm.at[pl.ds(keep, n_tok)], out_sems.at[n_tok + n_cs])])
    return ins, outs


def _sample_step(weights, xs_buf, sc_buf, sp_buf, ncs_buf, nps_buf, y_prev, hp_buf):
    ng_ref, win_ref, cw_ref, pw_ref, ps_ref, wout_ref, fg_ref = weights
    n_tok, n_seq, _ = xs_buf.shape

    def tok(a, t):
        return a[t * n_seq:(t + 1) * n_seq, :]

    x = jnp.concatenate([xs_buf[t] for t in range(n_tok)], axis=0)
    h = _rmsnorm(x, ng_ref[...])
    _finish_tile(y_prev, hp_buf, fg_ref)
    v = _dot(h, win_ref[:, _COL_V:_COL_V + D_POOL])
    z_b = _dot(h, win_ref[:, _COL_ZB:_COL_ZB + D_POOL])
    proj = _dot(h, win_ref[:, _COL_B:_COL_ZA + D_CONV])

    vp = [sp_buf[k] for k in range(POOL_HIST)] + [tok(v, t) for t in range(n_tok)]
    for t in range(n_tok):
        nps_buf[t] = tok(v, t)
    pooled = []
    for gi, w in enumerate(POOL_WINDOWS):
        lo = gi * POOL_GC
        rows = []
        for t in range(n_tok):
            vg = vp[POOL_HIST + t][:, lo:lo + POOL_GC]
            acc = vg
            for k in range(1, w):
                acc = acc + vp[POOL_HIST + t - k][:, lo:lo + POOL_GC]
            rows.append(acc * (1.0 / w) - vg)
        pooled.append(jnp.concatenate(rows, axis=0))
    y_b = _pool_project(pooled, pw_ref) * ps_ref[...] * _silu(z_b)

    b_g, c_g, u, z_a = (proj[:, c:c + D_CONV] for c in (_COL_B, _COL_C, _COL_U, _COL_ZA))
    cu = c_g * u
    cw = [cw_ref[k] for k in range(CONV_W)]
    up = [sc_buf[k] for k in range(CONV_HIST)] + [tok(cu, t) for t in range(n_tok)]
    for k in range(CONV_HIST):
        ncs_buf[k] = up[n_tok + k]
    yc = []
    for t in range(n_tok):
        acc = cw[0] * up[t]
        for k in range(1, CONV_W):
            acc = acc + cw[k] * up[t + k]
        yc.append(acc)
    y_a = b_g * jnp.concatenate(yc, axis=0) * _silu(z_a)

    ycat = jnp.concatenate([y_a, y_b], axis=-1)
    y = _rmsnorm(x + _dot(ycat, wout_ref[...]), fg_ref[...])
    for t in range(n_tok):
        xs_buf[t] = tok(y, t)


def _mixer_kernel(x_ref, xs_hbm, sc_hbm, sp_hbm, win_hbm, wout_hbm, ng_ref, cw_ref, pw_ref, ps_ref, fg_ref,
                  y_hbm, nc_ref, np_ref, ys_hbm, ncs_hbm, nps_hbm,
                  win_ref, wout_ref, y_stage, hp_buf, cu_buf, v_buf, xs_buf, sc_buf, sp_buf, ncs_buf, nps_buf,
                  w_sems, y_sems, in_sems, out_sems, *, tiles_per_seq):
    i = pl.program_id(0)
    n_prompt = pl.num_programs(0) - 1
    weights = (ng_ref, win_ref, cw_ref, pw_ref, ps_ref, wout_ref, fg_ref)
    piece_w = D_CONV
    pieces = ([(win_hbm, win_ref, c) for c in (_COL_V, _COL_ZB, _COL_B, _COL_C, _COL_U, _COL_ZA)]
              + [(wout_hbm, wout_ref, c) for c in range(0, D_MODEL, piece_w)])
    stage_pieces = ((0,), (1,), (2, 3, 4, 5), tuple(range(6, len(pieces))))

    def w_slot(k):
        return y_stage.at[1, :, pl.ds((k % 2) * piece_w, piece_w)]

    def w_copy(k):
        src, _, c = pieces[k]
        return pltpu.make_async_copy(src.at[:, pl.ds(c, piece_w)], w_slot(k), w_sems.at[k % 2])

    def arrive(stage):
        for k in stage_pieces[stage]:
            w_copy(k).wait()
            _, dst, c = pieces[k]
            dst[:, c:c + piece_w] = w_slot(k)[...].astype(jnp.bfloat16)
            if k + 2 < len(pieces):
                w_copy(k + 2).start()

    ins, outs = _sample_copies(xs_hbm, sc_hbm, sp_hbm, ys_hbm, ncs_hbm, nps_hbm,
                               xs_buf, sc_buf, sp_buf, ncs_buf, nps_buf, in_sems, out_sems)

    tail_r0, tail_n = SUB_TILES[-1]

    def y_copy(p, tail):
        slot = p % 2
        r0, n = (tail_r0, tail_n) if tail else (0, tail_r0)
        return pltpu.make_async_copy(
            y_stage.at[slot, pl.ds(r0, n)],
            y_hbm.at[p // tiles_per_seq, pl.ds((p % tiles_per_seq) * SEQ_TILE + r0, n)],
            y_sems.at[2 * slot + int(tail)])

    @pl.when(i >= 2)
    def _():
        y_copy(i - 2, False).wait()

    @pl.when(i >= 3)
    def _():
        y_copy(i - 3, True).wait()

    y_cur, y_prev = y_stage.at[i % 2], y_stage.at[(i + 1) % 2]

    @pl.when(i == 0)
    def _():
        for cp in [w_copy(0), w_copy(1)] + ins:
            cp.start()
        _prompt_tile(0, 0, tiles_per_seq - 1, x_ref, weights, y_cur, y_prev, hp_buf, nc_ref, np_ref,
                     cu_buf, v_buf, arrive)
        y_copy(i, False).start()

    @pl.when((i > 0) & (i < n_prompt))
    def _():
        _prompt_tile(i // tiles_per_seq, i % tiles_per_seq, tiles_per_seq - 1,
                     x_ref, weights, y_cur, y_prev, hp_buf, nc_ref, np_ref, cu_buf, v_buf)
        y_copy(i, False).start()
        y_copy(i - 1, True).start()

    @pl.when(i == n_prompt)
    def _():
        for cp in ins:
            cp.wait()
        _sample_step(weights, xs_buf, sc_buf, sp_buf, ncs_buf, nps_buf, y_prev, hp_buf)
        y_copy(i - 1, True).start()
        for cp in outs:
            cp.start()
        for cp in outs:
            cp.wait()
        y_copy(i - 1, False).wait()
        y_copy(i - 2, True).wait()
        y_copy(i - 1, True).wait()


def _const_spec(shape):
    nd = len(shape)
    return pl.BlockSpec(shape, lambda *_: (0,) * nd, pipeline_mode=pl.Buffered(1))


def kernel(x_prompt, x_sample, state_conv, state_pool, norm_g, w_in, conv_w, pool_w, pool_scale,
           w_out, final_g):
    assert norm_g.shape[0] == 1, "single-layer trunk"
    batch, seq, _ = x_prompt.shape
    n_seq, n_tok, _ = x_sample.shape
    assert seq % SEQ_TILE == 0 and n_tok < POOL_HIST
    assert SUB_TILES[0][0] == 0 and sum(n for _, n in SUB_TILES) == SEQ_TILE
    assert all(a + n == b for (a, n), (b, _) in zip(SUB_TILES, SUB_TILES[1:]))
    assert all(n % BF16_TILE_ROWS == 0 and n >= POOL_MAX for _, n in SUB_TILES)
    tiles_per_seq = seq // SEQ_TILE
    n_prompt = batch * tiles_per_seq
    assert n_prompt >= 2, "the output copy schedule drains two tiles in the decode step"

    small = (norm_g, jnp.swapaxes(conv_w, 0, 1), pool_w[0], pool_scale, final_g.reshape(1, D_MODEL))

    def tile(i):
        p = jnp.minimum(i, n_prompt - 1)
        return p // tiles_per_seq, p % tiles_per_seq

    hbm = pl.BlockSpec(memory_space=pl.ANY)
    outs = pl.pallas_call(
        functools.partial(_mixer_kernel, tiles_per_seq=tiles_per_seq),
        grid=(n_prompt + 1,),
        in_specs=[pl.BlockSpec((1, SEQ_TILE, D_MODEL), lambda i: (*tile(i), 0))] + [hbm] * 5
        + [_const_spec(a.shape) for a in small],
        out_specs=[
            hbm,
            pl.BlockSpec((1, batch, CONV_HIST, D_CONV), lambda i: (0, 0, 0, 0)),
            pl.BlockSpec((POOL_HIST, batch, D_POOL), lambda i: (0, 0, 0)),
            hbm, hbm, hbm,
        ],
        out_shape=[
            jax.ShapeDtypeStruct((batch, seq, D_MODEL), _F32),
            jax.ShapeDtypeStruct((1, batch, CONV_HIST, D_CONV), _F32),
            jax.ShapeDtypeStruct((POOL_HIST, batch, D_POOL), _F32),
            jax.ShapeDtypeStruct((n_seq, n_tok, D_MODEL), _F32),
            jax.ShapeDtypeStruct((n_seq, CONV_HIST, D_CONV), _F32),
            jax.ShapeDtypeStruct((POOL_HIST, n_seq, D_POOL), _F32),
        ],
        scratch_shapes=[
            pltpu.VMEM((D_MODEL, D_IN), jnp.bfloat16),
            pltpu.VMEM((D_MODEL, D_MODEL), jnp.bfloat16),
            pltpu.VMEM((2, SEQ_TILE, D_MODEL), _F32),
            pltpu.VMEM((SUB_TILES[-1][1], D_MODEL), _F32),
            pltpu.VMEM((CONV_PAD + SEQ_TILE, D_CONV), _F32),
            pltpu.VMEM((POOL_PAD + SEQ_TILE, D_POOL), _F32),
            pltpu.VMEM((n_tok, n_seq, D_MODEL), _F32),
            pltpu.VMEM((CONV_HIST, n_seq, D_CONV), _F32),
            pltpu.VMEM((POOL_HIST, n_seq, D_POOL), _F32),
            pltpu.VMEM((CONV_HIST, n_seq, D_CONV), _F32),
            pltpu.VMEM((n_tok, n_seq, D_POOL), _F32),
            pltpu.SemaphoreType.DMA((2,)),
            pltpu.SemaphoreType.DMA((4,)),
            pltpu.SemaphoreType.DMA((n_tok + CONV_HIST + 2,)),
            pltpu.SemaphoreType.DMA((n_tok + CONV_HIST + 1,)),
        ],
        compiler_params=pltpu.CompilerParams(
            dimension_semantics=("arbitrary",),
            vmem_limit_bytes=VMEM_LIMIT_BYTES),
        name="mixer_step",
    )(x_prompt, x_sample, state_conv[0], jnp.swapaxes(state_pool[0], 0, 1), w_in[0], w_out[0], *small)
    y_prompt, nc_p, np_p, y_sample, nc_s, np_s = outs
    return (y_prompt, y_sample, nc_p, jnp.swapaxes(np_p, 0, 1)[None], nc_s[None],
            jnp.swapaxes(np_s, 0, 1)[None])
```

```python
import functools

import jax
import jax.numpy as jnp
from jax import lax
from jax.experimental import pallas as pl
from jax.experimental.pallas import tpu as pltpu

D_MODEL = 1024
D_CONV = 512
D_POOL = 512
POOL_WINDOWS = (2, 4, 8, 16)
POOL_GC = D_POOL // len(POOL_WINDOWS)
POOL_MAX = max(POOL_WINDOWS)
POOL_HIST = POOL_MAX - 1
CONV_W = 3
CONV_HIST = CONV_W - 1
D_IN = 4 * D_CONV + 2 * D_POOL
EPS = 1e-6

SEQ_TILE = 1024
SUB_TILES = ((0, 384), (384, 384), (768, 256))
BF16_TILE_ROWS = 16
CONV_PAD = 8
POOL_PAD = 16
VMEM_LIMIT_BYTES = 60 * 1024 * 1024

_F32 = jnp.float32

_COL_B, _COL_C, _COL_U, _COL_ZA = 0, D_CONV, 2 * D_CONV, 3 * D_CONV
_COL_V, _COL_ZB = 4 * D_CONV, 4 * D_CONV + D_POOL


def _rmsnorm(x, g):
    ms = jnp.mean(x * x, axis=-1, keepdims=True)
    return (x * lax.rsqrt(ms + EPS)) * g


def _silu(z):
    return z * jax.nn.sigmoid(z)


def _dot(a, b):
    return jnp.dot(a.astype(jnp.bfloat16), b.astype(jnp.bfloat16), preferred_element_type=_F32)


def _pool_project(pooled_groups, pw_ref):
    outs = [_dot(p, pw_ref[gi]) for gi, p in enumerate(pooled_groups)]
    return jnp.concatenate(outs, axis=-1)


def _window_sums(ext):
    assert all(b == 2 * a for a, b in zip((1,) + POOL_WINDOWS, POOL_WINDOWS)) and POOL_PAD >= POOL_MAX
    sums, width, out = ext, 1, []
    for w in POOL_WINDOWS:
        while width < w:
            sums = sums + pltpu.roll(sums, width, axis=0)
            width *= 2
        out.append(sums[POOL_PAD:, 0:POOL_GC])
        sums = sums[:, POOL_GC:]
    return out


def _finish_tile(y_stage, hp_buf, fg_ref):
    r0, n = SUB_TILES[-1]
    y_stage[r0:r0 + n, :] = _rmsnorm(hp_buf[...], fg_ref[...])


def _prompt_tile(b, j, last_j, x_ref, weights, y_cur, y_prev, hp_buf, nc_ref, np_ref, cu_buf, v_buf,
                 weight_copies=None):
    def arrive(k):
        if weight_copies is not None:
            weight_copies[k].wait()

    ng_ref, win_ref, cw_ref, pw_ref, ps_ref, wout_ref, fg_ref = weights
    ts = SEQ_TILE
    head = POOL_MAX

    @pl.when(j == 0)
    def _():
        cu_buf[0:CONV_PAD, :] = jnp.zeros((CONV_PAD, D_CONV), _F32)
        v_buf[0:POOL_PAD, :] = jnp.zeros((POOL_PAD, D_POOL), _F32)

    cw = [cw_ref[k] for k in range(CONV_W)]
    pos = j * ts + 1 + lax.broadcasted_iota(jnp.int32, (head, POOL_GC), 0)

    xs = [x_ref[0, r0:r0 + n, :] for r0, n in SUB_TILES]
    hs = [_rmsnorm(x, ng_ref[...]) for x in xs]
    if weight_copies is None:
        _finish_tile(y_prev, hp_buf, fg_ref)
    arrive(0)
    vs = [_dot(h, win_ref[:, _COL_V:_COL_V + D_POOL]) for h in hs]
    arrive(1)
    zbs = [_dot(h, win_ref[:, _COL_ZB:_COL_ZB + D_POOL]) for h in hs]
    arrive(2)
    projs = [_dot(h, win_ref[:, _COL_B:_COL_ZA + D_CONV]) for h in hs]

    pooled = []
    for s, v in enumerate(vs):
        r0, n = SUB_TILES[s]
        p0 = POOL_PAD + r0
        v_buf[p0:p0 + n, :] = v
        wsums = _window_sums(v_buf[p0 - POOL_PAD:p0 + n, :])
        groups = []
        for gi, w in enumerate(POOL_WINDOWS):
            acc = wsums[gi]
            if s == 0:
                cnt = jnp.minimum(pos, w).astype(_F32)
                mean = jnp.concatenate([acc[0:head] / cnt, acc[head:] * (1.0 / w)], axis=0)
            else:
                mean = acc * (1.0 / w)
            groups.append(mean - v[:, gi * POOL_GC:(gi + 1) * POOL_GC])
        pooled.append(groups)

    y_as = []
    for s, proj in enumerate(projs):
        b_g, c_g, u, z_a = (proj[:, c:c + D_CONV] for c in (_COL_B, _COL_C, _COL_U, _COL_ZA))
        cu = c_g * u
        r0, n = SUB_TILES[s]
        c0 = CONV_PAD + r0
        cu_buf[c0:c0 + n, :] = cu
        ext = cu_buf[c0 - CONV_PAD:c0 + n, :]
        yc = cw[0] * pltpu.roll(ext, CONV_HIST, axis=0)[CONV_PAD:]
        for k in range(1, CONV_HIST):
            yc = yc + cw[k] * pltpu.roll(ext, CONV_HIST - k, axis=0)[CONV_PAD:]
        yc = yc + cw[CONV_HIST] * cu
        y_as.append(b_g * yc * _silu(z_a))

    y_bs = [_pool_project(p, pw_ref) * ps_ref[...] * _silu(z_b) for p, z_b in zip(pooled, zbs)]
    arrive(3)
    mixes = [_dot(jnp.concatenate([y_a, y_b], axis=-1), wout_ref[...]) for y_a, y_b in zip(y_as, y_bs)]
    for (r0, n), x, mix in zip(SUB_TILES[:-1], xs, mixes):
        y_cur[r0:r0 + n, :] = _rmsnorm(x + mix, fg_ref[...])
    hp_buf[...] = xs[-1] + mixes[-1]

    @pl.when(j == last_j)
    def _():
        nc_ref[0, b] = cu_buf[CONV_PAD + ts - CONV_HIST:CONV_PAD + ts, :]

    tail0 = POOL_PAD + ts - POOL_HIST
    n_seq = np_ref.shape[1]

    @pl.when((j == last_j) & (b == 0))
    def _():
        for k in range(POOL_HIST):
            np_ref[k] = jnp.broadcast_to(v_buf[tail0 + k:tail0 + k + 1, :], (n_seq, D_POOL))

    @pl.when((j == last_j) & (b > 0))
    def _():
        mine = lax.broadcasted_iota(jnp.int32, (n_seq, D_POOL), 0) == b
        for k in range(POOL_HIST):
            row = jnp.broadcast_to(v_buf[tail0 + k:tail0 + k + 1, :], (n_seq, D_POOL))
            np_ref[k] = jnp.where(mine, row, np_ref[k])

    cu_buf[0:CONV_PAD, :] = cu_buf[ts:ts + CONV_PAD, :]
    v_buf[0:POOL_PAD, :] = v_buf[ts:ts + POOL_PAD, :]


def _sample_copies(xs_hbm, sc_hbm, sp_hbm, ys_hbm, ncs_hbm, nps_hbm,
                   xs_buf, sc_buf, sp_buf, ncs_buf, nps_buf, in_sems, out_sems):
    n_tok, n_cs = xs_buf.shape[0], sc_buf.shape[0]
    keep = POOL_HIST - n_tok
    ins = ([pltpu.make_async_copy(xs_hbm.at[:, t, :], xs_buf.at[t], in_sems.at[t]) for t in range(n_tok)]
           + [pltpu.make_async_copy(sc_hbm.at[:, k, :], sc_buf.at[k], in_sems.at[n_tok + k])
              for k in range(n_cs)]
           + [pltpu.make_async_copy(sp_hbm, sp_buf, in_sems.at[n_tok + n_cs]),
              pltpu.make_async_copy(sp_hbm.at[pl.ds(n_tok, keep)], nps_hbm.at[pl.ds(0, keep)],
                                    in_sems.at[n_tok + n_cs + 1])])
    outs = ([pltpu.make_async_copy(xs_buf.at[t], ys_hbm.at[:, t, :], out_sems.at[t]) for t in range(n_tok)]
            + [pltpu.make_async_copy(ncs_buf.at[k], ncs_hbm.at[:, k, :], out_sems.at[n_tok + k])
               for k in range(n_cs)]
            + [pltpu.make_async_copy(nps_buf, nps_hbm.at[pl.ds(keep, n_tok)], out_sems.at[n_tok + n_cs])])
    return ins, outs


def _sample_step(weights, xs_buf, sc_buf, sp_buf, ncs_buf, nps_buf, y_prev, hp_buf):
    ng_ref, win_ref, cw_ref, pw_ref, ps_ref, wout_ref, fg_ref = weights
    n_tok, n_seq, _ = xs_buf.shape

    def tok(a, t):
        return a[t * n_seq:(t + 1) * n_seq, :]

    x = jnp.concatenate([xs_buf[t] for t in range(n_tok)], axis=0)
    h = _rmsnorm(x, ng_ref[...])
    _finish_tile(y_prev, hp_buf, fg_ref)
    v = _dot(h, win_ref[:, _COL_V:_COL_V + D_POOL])
    z_b = _dot(h, win_ref[:, _COL_ZB:_COL_ZB + D_POOL])
    proj = _dot(h, win_ref[:, _COL_B:_COL_ZA + D_CONV])

    vp = [sp_buf[k] for k in range(POOL_HIST)] + [tok(v, t) for t in range(n_tok)]
    for t in range(n_tok):
        nps_buf[t] = tok(v, t)
    pooled = []
    for gi, w in enumerate(POOL_WINDOWS):
        lo = gi * POOL_GC
        rows = []
        for t in range(n_tok):
            vg = vp[POOL_HIST + t][:, lo:lo + POOL_GC]
            acc = vg
            for k in range(1, w):
                acc = acc + vp[POOL_HIST + t - k][:, lo:lo + POOL_GC]
            rows.append(acc * (1.0 / w) - vg)
        pooled.append(jnp.concatenate(rows, axis=0))
    y_b = _pool_project(pooled, pw_ref) * ps_ref[...] * _silu(z_b)

    b_g, c_g, u, z_a = (proj[:, c:c + D_CONV] for c in (_COL_B, _COL_C, _COL_U, _COL_ZA))
    cu = c_g * u
    cw = [cw_ref[k] for k in range(CONV_W)]
    up = [sc_buf[k] for k in range(CONV_HIST)] + [tok(cu, t) for t in range(n_tok)]
    for k in range(CONV_HIST):
        ncs_buf[k] = up[n_tok + k]
    yc = []
    for t in range(n_tok):
        acc = cw[0] * up[t]
        for k in range(1, CONV_W):
            acc = acc + cw[k] * up[t + k]
        yc.append(acc)
    y_a = b_g * jnp.concatenate(yc, axis=0) * _silu(z_a)

    ycat = jnp.concatenate([y_a, y_b], axis=-1)
    y = _rmsnorm(x + _dot(ycat, wout_ref[...]), fg_ref[...])
    for t in range(n_tok):
        xs_buf[t] = tok(y, t)


def _mixer_kernel(x_ref, xs_hbm, sc_hbm, sp_hbm, win_hbm, wout_hbm, ng_ref, cw_ref, pw_ref, ps_ref, fg_ref,
                  y_hbm, nc_ref, np_ref, ys_hbm, ncs_hbm, nps_hbm,
                  win_ref, wout_ref, y_stage, hp_buf, cu_buf, v_buf, xs_buf, sc_buf, sp_buf, ncs_buf, nps_buf,
                  w_sems, y_sems, in_sems, out_sems, *, tiles_per_seq):
    i = pl.program_id(0)
    n_prompt = pl.num_programs(0) - 1
    weights = (ng_ref, win_ref, cw_ref, pw_ref, ps_ref, wout_ref, fg_ref)
    w_cols = ((_COL_V, D_POOL), (_COL_ZB, D_POOL), (_COL_B, 4 * D_CONV))
    weight_copies = [pltpu.make_async_copy(win_hbm.at[:, pl.ds(c, n)], win_ref.at[:, pl.ds(c, n)],
                                           w_sems.at[k]) for k, (c, n) in enumerate(w_cols)]
    weight_copies.append(pltpu.make_async_copy(wout_hbm, wout_ref, w_sems.at[len(w_cols)]))
    ins, outs = _sample_copies(xs_hbm, sc_hbm, sp_hbm, ys_hbm, ncs_hbm, nps_hbm,
                               xs_buf, sc_buf, sp_buf, ncs_buf, nps_buf, in_sems, out_sems)

    tail_r0, tail_n = SUB_TILES[-1]

    def y_copy(p, tail):
        slot = p % 2
        r0, n = (tail_r0, tail_n) if tail else (0, tail_r0)
        return pltpu.make_async_copy(
            y_stage.at[slot, pl.ds(r0, n)],
            y_hbm.at[p // tiles_per_seq, pl.ds((p % tiles_per_seq) * SEQ_TILE + r0, n)],
            y_sems.at[2 * slot + int(tail)])

    @pl.when(i >= 2)
    def _():
        y_copy(i - 2, False).wait()

    @pl.when(i >= 3)
    def _():
        y_copy(i - 3, True).wait()

    y_cur, y_prev = y_stage.at[i % 2], y_stage.at[(i + 1) % 2]

    @pl.when(i == 0)
    def _():
        for cp in weight_copies + ins:
            cp.start()
        _prompt_tile(0, 0, tiles_per_seq - 1, x_ref, weights, y_cur, y_prev, hp_buf, nc_ref, np_ref,
                     cu_buf, v_buf, weight_copies)
        y_copy(i, False).start()

    @pl.when((i > 0) & (i < n_prompt))
    def _():
        _prompt_tile(i // tiles_per_seq, i % tiles_per_seq, tiles_per_seq - 1,
                     x_ref, weights, y_cur, y_prev, hp_buf, nc_ref, np_ref, cu_buf, v_buf)
        y_copy(i, False).start()
        y_copy(i - 1, True).start()

    @pl.when(i == n_prompt)
    def _():
        for cp in ins:
            cp.wait()
        _sample_step(weights, xs_buf, sc_buf, sp_buf, ncs_buf, nps_buf, y_prev, hp_buf)
        y_copy(i - 1, True).start()
        for cp in outs:
            cp.start()
        for cp in outs:
            cp.wait()
        y_copy(i - 1, False).wait()
        y_copy(i - 2, True).wait()
        y_copy(i - 1, True).wait()


def _const_spec(shape):
    nd = len(shape)
    return pl.BlockSpec(shape, lambda *_: (0,) * nd, pipeline_mode=pl.Buffered(1))


def kernel(x_prompt, x_sample, state_conv, state_pool, norm_g, w_in, conv_w, pool_w, pool_scale,
           w_out, final_g):
    assert norm_g.shape[0] == 1, "single-layer trunk"
    batch, seq, _ = x_prompt.shape
    n_seq, n_tok, _ = x_sample.shape
    assert seq % SEQ_TILE == 0 and n_tok < POOL_HIST and CONV_HIST <= CONV_PAD
    assert SUB_TILES[0][0] == 0 and sum(n for _, n in SUB_TILES) == SEQ_TILE
    assert all(a + n == b for (a, n), (b, _) in zip(SUB_TILES, SUB_TILES[1:]))
    assert all(n % BF16_TILE_ROWS == 0 and n >= POOL_MAX for _, n in SUB_TILES)
    tiles_per_seq = seq // SEQ_TILE
    n_prompt = batch * tiles_per_seq
    assert n_prompt >= 2, "the output copy schedule drains two tiles in the decode step"

    small = (norm_g, jnp.swapaxes(conv_w, 0, 1), pool_w[0], pool_scale, final_g.reshape(1, D_MODEL))

    def tile(i):
        p = jnp.minimum(i, n_prompt - 1)
        return p // tiles_per_seq, p % tiles_per_seq

    hbm = pl.BlockSpec(memory_space=pl.ANY)
    outs = pl.pallas_call(
        functools.partial(_mixer_kernel, tiles_per_seq=tiles_per_seq),
        grid=(n_prompt + 1,),
        in_specs=[pl.BlockSpec((1, SEQ_TILE, D_MODEL), lambda i: (*tile(i), 0))] + [hbm] * 5
        + [_const_spec(a.shape) for a in small],
        out_specs=[
            hbm,
            pl.BlockSpec((1, batch, CONV_HIST, D_CONV), lambda i: (0, 0, 0, 0)),
            pl.BlockSpec((POOL_HIST, batch, D_POOL), lambda i: (0, 0, 0)),
            hbm, hbm, hbm,
        ],
        out_shape=[
            jax.ShapeDtypeStruct((batch, seq, D_MODEL), _F32),
            jax.ShapeDtypeStruct((1, batch, CONV_HIST, D_CONV), _F32),
            jax.ShapeDtypeStruct((POOL_HIST, batch, D_POOL), _F32),
            jax.ShapeDtypeStruct((n_seq, n_tok, D_MODEL), _F32),
            jax.ShapeDtypeStruct((n_seq, CONV_HIST, D_CONV), _F32),
            jax.ShapeDtypeStruct((POOL_HIST, n_seq, D_POOL), _F32),
        ],
        scratch_shapes=[
            pltpu.VMEM((D_MODEL, D_IN), _F32),
            pltpu.VMEM((D_MODEL, D_MODEL), _F32),
            pltpu.VMEM((2, SEQ_TILE, D_MODEL), _F32),
            pltpu.VMEM((SUB_TILES[-1][1], D_MODEL), _F32),
            pltpu.VMEM((CONV_PAD + SEQ_TILE, D_CONV), _F32),
            pltpu.VMEM((POOL_PAD + SEQ_TILE, D_POOL), _F32),
            pltpu.VMEM((n_tok, n_seq, D_MODEL), _F32),
            pltpu.VMEM((CONV_HIST, n_seq, D_CONV), _F32),
            pltpu.VMEM((POOL_HIST, n_seq, D_POOL), _F32),
            pltpu.VMEM((CONV_HIST, n_seq, D_CONV), _F32),
            pltpu.VMEM((n_tok, n_seq, D_POOL), _F32),
            pltpu.SemaphoreType.DMA((4,)),
            pltpu.SemaphoreType.DMA((4,)),
            pltpu.SemaphoreType.DMA((n_tok + CONV_HIST + 2,)),
            pltpu.SemaphoreType.DMA((n_tok + CONV_HIST + 1,)),
        ],
        compiler_params=pltpu.CompilerParams(
            dimension_semantics=("arbitrary",),
            vmem_limit_bytes=VMEM_LIMIT_BYTES),
        name="mixer_step",
    )(x_prompt, x_sample, state_conv[0], jnp.swapaxes(state_pool[0], 0, 1), w_in[0], w_out[0], *small)
    y_prompt, nc_p, np_p, y_sample, nc_s, np_s = outs
    return (y_prompt, y_sample, nc_p, jnp.swapaxes(np_p, 0, 1)[None], nc_s[None],
            jnp.swapaxes(np_s, 0, 1)[None])
```

```python
import functools

import jax
import jax.numpy as jnp
from jax import lax
from jax.experimental import pallas as pl
from jax.experimental.pallas import tpu as pltpu

D_MODEL = 1024
D_CONV = 512
D_POOL = 512
POOL_WINDOWS = (2, 4, 8, 16)
POOL_GC = D_POOL // len(POOL_WINDOWS)
POOL_MAX = max(POOL_WINDOWS)
POOL_HIST = POOL_MAX - 1
CONV_W = 3
CONV_HIST = CONV_W - 1
D_IN = 4 * D_CONV + 2 * D_POOL
EPS = 1e-6

SEQ_TILE = 2048
HALF_TILE = 1024
SUB_TILES = ((0, 384), (384, 384), (768, 256))
BF16_TILE_ROWS = 16
CONV_PAD = 8
POOL_PAD = 16
VMEM_LIMIT_BYTES = 63 * 1024 * 1024 + 768 * 1024

_F32 = jnp.float32

_COL_B, _COL_C, _COL_U, _COL_ZA = 0, D_CONV, 2 * D_CONV, 3 * D_CONV
_COL_V, _COL_ZB = 4 * D_CONV, 4 * D_CONV + D_POOL


def _rmsnorm(x, g):
    ms = jnp.mean(x * x, axis=-1, keepdims=True)
    return (x * lax.rsqrt(ms + EPS)) * g


def _silu(z):
    return z * jax.nn.sigmoid(z)


def _dot(a, b):
    return jnp.dot(a.astype(jnp.bfloat16), b.astype(jnp.bfloat16), preferred_element_type=_F32)


def _pool_project(pooled_groups, pw_ref):
    outs = [_dot(p, pw_ref[gi]) for gi, p in enumerate(pooled_groups)]
    return jnp.concatenate(outs, axis=-1)


def _window_sums(ext):
    assert all(b == 2 * a for a, b in zip((1,) + POOL_WINDOWS, POOL_WINDOWS)) and POOL_PAD >= POOL_MAX
    sums, width, out = ext, 1, []
    for w in POOL_WINDOWS:
        while width < w:
            sums = sums + pltpu.roll(sums, width, axis=0)
            width *= 2
        out.append(sums[POOL_PAD:, 0:POOL_GC])
        sums = sums[:, POOL_GC:]
    return out


def _finish_tile(y_stage, hp_buf, fg_ref):
    r0, n = SUB_TILES[-1]
    r0 += SEQ_TILE - HALF_TILE
    y_stage[r0:r0 + n, :] = _rmsnorm(hp_buf[...], fg_ref[...])


def _prompt_tile(b, j, last_j, x_ref, weights, y_cur, y_prev, hp_buf, nc_ref, np_ref, cu_buf, v_buf,
                 arrive=None):
    first = arrive is not None
    if not first:
        def arrive(k):
            pass

    ng_ref, win_ref, cw_ref, pw_ref, ps_ref, wout_ref, fg_ref = weights
    ts = SEQ_TILE
    head = POOL_MAX

    @pl.when(j == 0)
    def _():
        cu_buf[0:CONV_PAD, :] = jnp.zeros((CONV_PAD, D_CONV), _F32)
        v_buf[0:POOL_PAD, :] = jnp.zeros((POOL_PAD, D_POOL), _F32)

    cw = [cw_ref[k] for k in range(CONV_W)]
    n_half = ts // HALF_TILE
    n_seq = np_ref.shape[1]

    for hf in range(n_half):
        h0 = hf * HALF_TILE
        last_half = hf == n_half - 1
        waits = arrive if hf == 0 else (lambda k: None)
        pos = j * ts + h0 + 1 + lax.broadcasted_iota(jnp.int32, (head, POOL_GC), 0)

        xs = [x_ref[0, h0 + r0:h0 + r0 + n, :] for r0, n in SUB_TILES]
        hs = [_rmsnorm(x, ng_ref[...]) for x in xs]
        if hf == 0 and not first:
            _finish_tile(y_prev, hp_buf, fg_ref)
        waits(0)
        vs = [_dot(h, win_ref[:, _COL_V:_COL_V + D_POOL]) for h in hs]
        waits(1)
        zbs = [_dot(h, win_ref[:, _COL_ZB:_COL_ZB + D_POOL]) for h in hs]
        waits(2)
        projs = [_dot(h, win_ref[:, _COL_B:_COL_ZA + D_CONV]) for h in hs]

        pooled = []
        for s, v in enumerate(vs):
            r0, n = SUB_TILES[s]
            p0 = POOL_PAD + r0
            v_buf[p0:p0 + n, :] = v
            wsums = _window_sums(v_buf[p0 - POOL_PAD:p0 + n, :])
            groups = []
            for gi, w in enumerate(POOL_WINDOWS):
                acc = wsums[gi]
                if s == 0 and hf == 0:
                    cnt = jnp.minimum(pos, w).astype(_F32)
                    mean = jnp.concatenate([acc[0:head] / cnt, acc[head:] * (1.0 / w)], axis=0)
                else:
                    mean = acc * (1.0 / w)
                groups.append(mean - v[:, gi * POOL_GC:(gi + 1) * POOL_GC])
            pooled.append(groups)

        y_as = []
        for s, proj in enumerate(projs):
            b_g, c_g, u, z_a = (proj[:, c:c + D_CONV] for c in (_COL_B, _COL_C, _COL_U, _COL_ZA))
            cu = c_g * u
            r0, n = SUB_TILES[s]
            c0 = CONV_PAD + r0
            cu_buf[c0:c0 + n, :] = cu
            ext = cu_buf[c0 - CONV_PAD:c0 + n, :]
            yc = cw[0] * pltpu.roll(ext, CONV_HIST, axis=0)[CONV_PAD:]
            for k in range(1, CONV_HIST):
                yc = yc + cw[k] * pltpu.roll(ext, CONV_HIST - k, axis=0)[CONV_PAD:]
            yc = yc + cw[CONV_HIST] * cu
            y_as.append(b_g * yc * _silu(z_a))

        y_bs = [_pool_project(p, pw_ref) * ps_ref[...] * _silu(z_b) for p, z_b in zip(pooled, zbs)]
        waits(3)
        mixes = [_dot(jnp.concatenate([y_a, y_b], axis=-1), wout_ref[...]) for y_a, y_b in zip(y_as, y_bs)]
        n_now = len(SUB_TILES) - 1 if last_half else len(SUB_TILES)
        for (r0, n), x, mix in list(zip(SUB_TILES, xs, mixes))[:n_now]:
            y_cur[h0 + r0:h0 + r0 + n, :] = _rmsnorm(x + mix, fg_ref[...])
        if last_half:
            hp_buf[...] = xs[-1] + mixes[-1]

            @pl.when(j == last_j)
            def _():
                nc_ref[0, b] = cu_buf[CONV_PAD + HALF_TILE - CONV_HIST:CONV_PAD + HALF_TILE, :]

            tail0 = POOL_PAD + HALF_TILE - POOL_HIST

            @pl.when((j == last_j) & (b == 0))
            def _():
                for k in range(POOL_HIST):
                    np_ref[k] = jnp.broadcast_to(v_buf[tail0 + k:tail0 + k + 1, :], (n_seq, D_POOL))

            @pl.when((j == last_j) & (b > 0))
            def _():
                mine = lax.broadcasted_iota(jnp.int32, (n_seq, D_POOL), 0) == b
                for k in range(POOL_HIST):
                    row = jnp.broadcast_to(v_buf[tail0 + k:tail0 + k + 1, :], (n_seq, D_POOL))
                    np_ref[k] = jnp.where(mine, row, np_ref[k])

        cu_buf[0:CONV_PAD, :] = cu_buf[HALF_TILE:HALF_TILE + CONV_PAD, :]
        v_buf[0:POOL_PAD, :] = v_buf[HALF_TILE:HALF_TILE + POOL_PAD, :]


def _sample_copies(xs_hbm, sc_hbm, sp_hbm, ys_hbm, ncs_hbm, nps_hbm,
                   xs_buf, sc_buf, sp_bufs, nps_buf, in_sems, out_sems):
    n_tok, n_seq, n_cs = xs_buf.shape[0], xs_buf.shape[1], sc_buf.shape[0]
    keep = POOL_HIST - n_tok
    ins = ([pltpu.make_async_copy(xs_hbm.at[:, t, :], xs_buf.at[t], in_sems.at[t]) for t in range(n_tok)]
           + [pltpu.make_async_copy(sc_hbm.at[:, k, :], sc_buf.at[k], in_sems.at[n_tok + k])
              for k in range(n_cs)]
           + [pltpu.make_async_copy(
               sp_hbm.at[pl.ds(POOL_HIST - (w - 1), w - 1), :, pl.ds(gi * POOL_GC, POOL_GC)],
               sp_bufs[gi], in_sems.at[n_tok + n_cs + gi]) for gi, w in enumerate(POOL_WINDOWS)]
           + [pltpu.make_async_copy(sp_hbm.at[pl.ds(n_tok, keep)], nps_hbm.at[pl.ds(0, keep)],
                                    in_sems.at[n_tok + n_cs + len(POOL_WINDOWS)])])
    outs = ([pltpu.make_async_copy(xs_buf.at[t], ys_hbm.at[:, t, :], out_sems.at[t]) for t in range(n_tok)]
            + [pltpu.make_async_copy(sc_buf.at[k], ncs_hbm.at[:, k, :], out_sems.at[n_tok + k])
               for k in range(n_cs)]
            + [pltpu.make_async_copy(nps_buf.at[pl.ds(t * n_seq, n_seq)], nps_hbm.at[keep + t],
                                     out_sems.at[n_tok + n_cs + t]) for t in range(n_tok)])
    return ins, outs


def _sample_step(weights, xs_buf, sc_buf, sp_bufs, nps_buf, y_prev, hp_buf):
    ng_ref, win_ref, cw_ref, pw_ref, ps_ref, wout_ref, fg_ref = weights
    n_tok, n_seq, _ = xs_buf.shape

    def tok(a, t):
        return a[t * n_seq:(t + 1) * n_seq, :]

    x = jnp.concatenate([xs_buf[t] for t in range(n_tok)], axis=0)
    h = _rmsnorm(x, ng_ref[...])
    _finish_tile(y_prev, hp_buf, fg_ref)
    v = _dot(h, win_ref[:, _COL_V:_COL_V + D_POOL])
    z_b = _dot(h, win_ref[:, _COL_ZB:_COL_ZB + D_POOL])
    proj = _dot(h, win_ref[:, _COL_B:_COL_ZA + D_CONV])

    nps_buf[0:n_tok * n_seq, :] = v
    pooled = []
    for gi, w in enumerate(POOL_WINDOWS):
        lo = gi * POOL_GC
        vp = [sp_bufs[gi][k] for k in range(w - 1)] + [tok(v, t)[:, lo:lo + POOL_GC] for t in range(n_tok)]
        rows = []
        for t in range(n_tok):
            vg = vp[w - 1 + t]
            acc = vg
            for k in range(1, w):
                acc = acc + vp[w - 1 + t - k]
            rows.append(acc * (1.0 / w) - vg)
        pooled.append(jnp.concatenate(rows, axis=0))
    y_b = _pool_project(pooled, pw_ref) * ps_ref[...] * _silu(z_b)

    b_g, c_g, u, z_a = (proj[:, c:c + D_CONV] for c in (_COL_B, _COL_C, _COL_U, _COL_ZA))
    cu = c_g * u
    cw = [cw_ref[k] for k in range(CONV_W)]
    up = [sc_buf[k] for k in range(CONV_HIST)] + [tok(cu, t) for t in range(n_tok)]
    for k in range(CONV_HIST):
        sc_buf[k] = up[n_tok + k]
    yc = []
    for t in range(n_tok):
        acc = cw[0] * up[t]
        for k in range(1, CONV_W):
            acc = acc + cw[k] * up[t + k]
        yc.append(acc)
    y_a = b_g * jnp.concatenate(yc, axis=0) * _silu(z_a)

    ycat = jnp.concatenate([y_a, y_b], axis=-1)
    y = _rmsnorm(x + _dot(ycat, wout_ref[...]), fg_ref[...])
    for t in range(n_tok):
        xs_buf[t] = tok(y, t)


def _mixer_kernel(x_ref, xs_hbm, sc_hbm, sp_hbm, win_hbm, wout_hbm, ng_ref, cw_ref, pw_ref, ps_ref, fg_ref,
                  y_hbm, nc_ref, np_ref, ys_hbm, ncs_hbm, nps_hbm,
                  win_ref, wout_ref, y_stage, hp_buf, cu_buf, v_buf, xs_buf, sc_buf, sp_g0, sp_g1, sp_g2, sp_g3,
                  w_sems, y_sems, in_sems, out_sems, *, tiles_per_seq):
    i = pl.program_id(0)
    n_prompt = pl.num_programs(0) - 1
    weights = (ng_ref, win_ref, cw_ref, pw_ref, ps_ref, wout_ref, fg_ref)
    piece_w = D_CONV
    col_slots = D_MODEL // piece_w
    n_slots = 2 * (SEQ_TILE // D_MODEL) * col_slots
    pieces = ([(win_hbm, win_ref, c) for c in (_COL_V, _COL_ZB, _COL_B, _COL_C, _COL_U, _COL_ZA)]
              + [(wout_hbm, wout_ref, c) for c in range(0, D_MODEL, piece_w)])
    stage_pieces = ((0,), (1,), (2, 3, 4, 5), tuple(range(6, len(pieces))))

    def w_slot(k):
        q = k % n_slots
        buf, q = divmod(q, n_slots // 2)
        row, col = divmod(q, col_slots)
        return y_stage.at[buf, pl.ds(row * D_MODEL, D_MODEL), pl.ds(col * piece_w, piece_w)]

    def w_copy(k):
        src, _, c = pieces[k]
        return pltpu.make_async_copy(src.at[:, pl.ds(c, piece_w)], w_slot(k), w_sems.at[k % n_slots])

    def arrive(stage):
        for k in stage_pieces[stage]:
            w_copy(k).wait()
            _, dst, c = pieces[k]
            dst[:, c:c + piece_w] = w_slot(k)[...].astype(jnp.bfloat16)
            if k + n_slots < len(pieces):
                w_copy(k + n_slots).start()

    sp_bufs = (sp_g0, sp_g1, sp_g2, sp_g3)
    ins, outs = _sample_copies(xs_hbm, sc_hbm, sp_hbm, ys_hbm, ncs_hbm, nps_hbm,
                               xs_buf, sc_buf, sp_bufs, v_buf, in_sems, out_sems)

    tail_r0, tail_n = SUB_TILES[-1][0] + SEQ_TILE - HALF_TILE, SUB_TILES[-1][1]

    def y_copy(p, tail):
        slot = p % 2
        r0, n = (tail_r0, tail_n) if tail else (0, tail_r0)
        return pltpu.make_async_copy(
            y_stage.at[slot, pl.ds(r0, n)],
            y_hbm.at[p // tiles_per_seq, pl.ds((p % tiles_per_seq) * SEQ_TILE + r0, n)],
            y_sems.at[2 * slot + int(tail)])

    @pl.when(i >= 2)
    def _():
        y_copy(i - 2, False).wait()

    @pl.when(i >= 3)
    def _():
        y_copy(i - 3, True).wait()

    y_cur, y_prev = y_stage.at[i % 2], y_stage.at[(i + 1) % 2]

    @pl.when(i == 0)
    def _():
        for cp in [w_copy(k) for k in range(n_slots)] + ins:
            cp.start()
        _prompt_tile(0, 0, tiles_per_seq - 1, x_ref, weights, y_cur, y_prev, hp_buf, nc_ref, np_ref,
                     cu_buf, v_buf, arrive)
        y_copy(i, False).start()

    @pl.when((i > 0) & (i < n_prompt))
    def _():
        _prompt_tile(i // tiles_per_seq, i % tiles_per_seq, tiles_per_seq - 1,
                     x_ref, weights, y_cur, y_prev, hp_buf, nc_ref, np_ref, cu_buf, v_buf)
        y_copy(i, False).start()
        y_copy(i - 1, True).start()

    @pl.when(i == n_prompt)
    def _():
        for cp in ins:
            cp.wait()
        _sample_step(weights, xs_buf, sc_buf, sp_bufs, v_buf, y_prev, hp_buf)
        y_copy(i - 1, True).start()
        for cp in outs:
            cp.start()
        for cp in outs:
            cp.wait()
        y_copy(i - 1, False).wait()
        y_copy(i - 2, True).wait()
        y_copy(i - 1, True).wait()


def _const_spec(shape):
    nd = len(shape)
    return pl.BlockSpec(shape, lambda *_: (0,) * nd, pipeline_mode=pl.Buffered(1))


def kernel(x_prompt, x_sample, state_conv, state_pool, norm_g, w_in, conv_w, pool_w, pool_scale,
           w_out, final_g):
    assert norm_g.shape[0] == 1, "single-layer trunk"
    batch, seq, _ = x_prompt.shape
    n_seq, n_tok, _ = x_sample.shape
    assert seq % SEQ_TILE == 0 and n_tok < POOL_HIST and CONV_HIST <= CONV_PAD
    assert n_tok * n_seq <= HALF_TILE, "the new decode pooling rows are staged in the prompt pooling scratch"
    assert SUB_TILES[0][0] == 0 and sum(n for _, n in SUB_TILES) == HALF_TILE and SEQ_TILE % HALF_TILE == 0
    assert all(a + n == b for (a, n), (b, _) in zip(SUB_TILES, SUB_TILES[1:]))
    assert all(n % BF16_TILE_ROWS == 0 and n >= POOL_MAX for _, n in SUB_TILES)
    tiles_per_seq = seq // SEQ_TILE
    n_prompt = batch * tiles_per_seq
    assert n_prompt >= 2, "the output copy schedule drains two tiles in the decode step"

    small = (norm_g, jnp.swapaxes(conv_w, 0, 1), pool_w[0], pool_scale, final_g.reshape(1, D_MODEL))

    def tile(i):
        p = jnp.minimum(i, n_prompt - 1)
        return p // tiles_per_seq, p % tiles_per_seq

    hbm = pl.BlockSpec(memory_space=pl.ANY)
    outs = pl.pallas_call(
        functools.partial(_mixer_kernel, tiles_per_seq=tiles_per_seq),
        grid=(n_prompt + 1,),
        in_specs=[pl.BlockSpec((1, SEQ_TILE, D_MODEL), lambda i: (*tile(i), 0))] + [hbm] * 5
        + [_const_spec(a.shape) for a in small],
        out_specs=[
            hbm,
            pl.BlockSpec((1, batch, CONV_HIST, D_CONV), lambda i: (0, 0, 0, 0)),
            pl.BlockSpec((POOL_HIST, batch, D_POOL), lambda i: (0, 0, 0)),
            hbm, hbm, hbm,
        ],
        out_shape=[
            jax.ShapeDtypeStruct((batch, seq, D_MODEL), _F32),
            jax.ShapeDtypeStruct((1, batch, CONV_HIST, D_CONV), _F32),
            jax.ShapeDtypeStruct((POOL_HIST, batch, D_POOL), _F32),
            jax.ShapeDtypeStruct((n_seq, n_tok, D_MODEL), _F32),
            jax.ShapeDtypeStruct((n_seq, CONV_HIST, D_CONV), _F32),
            jax.ShapeDtypeStruct((POOL_HIST, n_seq, D_POOL), _F32),
        ],
        scratch_shapes=[
            pltpu.VMEM((D_MODEL, D_IN), jnp.bfloat16),
            pltpu.VMEM((D_MODEL, D_MODEL), jnp.bfloat16),
            pltpu.VMEM((2, SEQ_TILE, D_MODEL), _F32),
            pltpu.VMEM((SUB_TILES[-1][1], D_MODEL), _F32),
            pltpu.VMEM((CONV_PAD + HALF_TILE, D_CONV), _F32),
            pltpu.VMEM((POOL_PAD + HALF_TILE, D_POOL), _F32),
            pltpu.VMEM((n_tok, n_seq, D_MODEL), _F32),
            pltpu.VMEM((CONV_HIST, n_seq, D_CONV), _F32),
        ] + [pltpu.VMEM((w - 1, n_seq, POOL_GC), _F32) for w in POOL_WINDOWS] + [
            pltpu.SemaphoreType.DMA((8,)),
            pltpu.SemaphoreType.DMA((4,)),
            pltpu.SemaphoreType.DMA((n_tok + CONV_HIST + len(POOL_WINDOWS) + 1,)),
            pltpu.SemaphoreType.DMA((2 * n_tok + CONV_HIST,)),
        ],
        compiler_params=pltpu.CompilerParams(
            dimension_semantics=("arbitrary",),
            vmem_limit_bytes=VMEM_LIMIT_BYTES),
        name="mixer_step",
    )(x_prompt, x_sample, state_conv[0], jnp.swapaxes(state_pool[0], 0, 1), w_in[0], w_out[0], *small)
    y_prompt, nc_p, np_p, y_sample, nc_s, np_s = outs
    return (y_prompt, y_sample, nc_p, jnp.swapaxes(np_p, 0, 1)[None], nc_s[None],
            jnp.swapaxes(np_s, 0, 1)[None])
```

```python
import functools

import jax
import jax.numpy as jnp
from jax import lax
from jax.experimental import pallas as pl
from jax.experimental.pallas import tpu as pltpu

D_MODEL = 1024
D_CONV = 512
D_POOL = 512
POOL_WINDOWS = (2, 4, 8, 16)
POOL_GC = D_POOL // len(POOL_WINDOWS)
POOL_MAX = max(POOL_WINDOWS)
POOL_HIST = POOL_MAX - 1
CONV_W = 3
CONV_HIST = CONV_W - 1
D_IN = 4 * D_CONV + 2 * D_POOL
EPS = 1e-6

SEQ_TILE = 1024
SUB_TILES = ((0, 384), (384, 384), (768, 256))
BF16_TILE_ROWS = 16
CONV_PAD = 8
POOL_PAD = 16
VMEM_LIMIT_BYTES = 60 * 1024 * 1024

_F32 = jnp.float32

_COL_B, _COL_C, _COL_U, _COL_ZA = 0, D_CONV, 2 * D_CONV, 3 * D_CONV
_COL_V, _COL_ZB = 4 * D_CONV, 4 * D_CONV + D_POOL
_CONV_PIECE_COLS = (_COL_B, _COL_U)


def _rmsnorm(x, g):
    ms = jnp.mean(x * x, axis=-1, keepdims=True)
    return (x * lax.rsqrt(ms + EPS)) * g


def _silu(z):
    return z * jax.nn.sigmoid(z)


def _dot(a, b):
    return jnp.dot(a.astype(jnp.bfloat16), b.astype(jnp.bfloat16), preferred_element_type=_F32)


def _pool_project(pooled_groups, pw_ref):
    outs = [_dot(p, pw_ref[gi]) for gi, p in enumerate(pooled_groups)]
    return jnp.concatenate(outs, axis=-1)


def _window_sums(ext):
    assert all(b == 2 * a for a, b in zip((1,) + POOL_WINDOWS, POOL_WINDOWS)) and POOL_PAD >= POOL_MAX
    sums, width, out = ext, 1, []
    for w in POOL_WINDOWS:
        while width < w:
            sums = sums + pltpu.roll(sums, width, axis=0)
            width *= 2
        out.append(sums[POOL_PAD:, 0:POOL_GC])
        sums = sums[:, POOL_GC:]
    return out


def _finish_tile(y_stage, hp_buf, fg_ref):
    r0, n = SUB_TILES[-1]
    y_stage[r0:r0 + n, :] = _rmsnorm(hp_buf[...], fg_ref[...])


def _prompt_tile(b, j, last_j, x_ref, weights, y_cur, y_prev, hp_buf, nc_ref, np_ref, cu_buf, v_buf,
                 weight_copies=None):
    def arrive(k):
        if weight_copies is not None:
            weight_copies[k].wait()

    ng_ref, win_ref, cw_ref, pw_ref, ps_ref, wout_ref, fg_ref = weights
    ts = SEQ_TILE
    head = POOL_MAX

    @pl.when(j == 0)
    def _():
        cu_buf[0:CONV_PAD, :] = jnp.zeros((CONV_PAD, D_CONV), _F32)
        v_buf[0:POOL_PAD, :] = jnp.zeros((POOL_PAD, D_POOL), _F32)

    cw = [cw_ref[k] for k in range(CONV_W)]
    pos = j * ts + 1 + lax.broadcasted_iota(jnp.int32, (head, POOL_GC), 0)

    xs = [x_ref[0, r0:r0 + n, :] for r0, n in SUB_TILES]
    hs = [_rmsnorm(x, ng_ref[...]) for x in xs]
    if weight_copies is None:
        _finish_tile(y_prev, hp_buf, fg_ref)
    arrive(0)
    vs = [_dot(h, win_ref[:, _COL_V:_COL_V + D_POOL]) for h in hs]
    arrive(1)
    zbs = [_dot(h, win_ref[:, _COL_ZB:_COL_ZB + D_POOL]) for h in hs]
    if weight_copies is None:
        projs = [_dot(h, win_ref[:, _COL_B:_COL_ZA + D_CONV]) for h in hs]
        parts = [[proj[:, c:c + D_CONV] for proj in projs] for c in (_COL_B, _COL_C, _COL_U, _COL_ZA)]
    else:
        parts = []
        for k, col in enumerate(_CONV_PIECE_COLS):
            arrive(2 + k)
            pair = [_dot(h, win_ref[:, col:col + 2 * D_CONV]) for h in hs]
            parts += [[p[:, 0:D_CONV] for p in pair], [p[:, D_CONV:] for p in pair]]
    b_gs, c_gs, us, z_as = parts

    pooled = []
    for s, v in enumerate(vs):
        r0, n = SUB_TILES[s]
        p0 = POOL_PAD + r0
        v_buf[p0:p0 + n, :] = v
        wsums = _window_sums(v_buf[p0 - POOL_PAD:p0 + n, :])
        groups = []
        for gi, w in enumerate(POOL_WINDOWS):
            acc = wsums[gi]
            if s == 0:
                cnt = jnp.minimum(pos, w).astype(_F32)
                mean = jnp.concatenate([acc[0:head] / cnt, acc[head:] * (1.0 / w)], axis=0)
            else:
                mean = acc * (1.0 / w)
            groups.append(mean - v[:, gi * POOL_GC:(gi + 1) * POOL_GC])
        pooled.append(groups)

    y_as = []
    for s, (r0, n) in enumerate(SUB_TILES):
        cu = c_gs[s] * us[s]
        c0 = CONV_PAD + r0
        cu_buf[c0:c0 + n, :] = cu
        ext = cu_buf[c0 - CONV_PAD:c0 + n, :]
        yc = cw[0] * pltpu.roll(ext, CONV_HIST, axis=0)[CONV_PAD:]
        for k in range(1, CONV_HIST):
            yc = yc + cw[k] * pltpu.roll(ext, CONV_HIST - k, axis=0)[CONV_PAD:]
        yc = yc + cw[CONV_HIST] * cu
        y_as.append(b_gs[s] * yc * _silu(z_as[s]))

    y_bs = [_pool_project(p, pw_ref) * ps_ref[...] * _silu(z_b) for p, z_b in zip(pooled, zbs)]
    arrive(2 + len(_CONV_PIECE_COLS))
    mixes = [_dot(jnp.concatenate([y_a, y_b], axis=-1), wout_ref[...]) for y_a, y_b in zip(y_as, y_bs)]
    for (r0, n), x, mix in zip(SUB_TILES[:-1], xs, mixes):
        y_cur[r0:r0 + n, :] = _rmsnorm(x + mix, fg_ref[...])
    hp_buf[...] = xs[-1] + mixes[-1]

    @pl.when(j == last_j)
    def _():
        nc_ref[0, b] = cu_buf[CONV_PAD + ts - CONV_HIST:CONV_PAD + ts, :]

    tail0 = POOL_PAD + ts - POOL_HIST
    n_seq = np_ref.shape[1]

    @pl.when((j == last_j) & (b == 0))
    def _():
        for k in range(POOL_HIST):
            np_ref[k] = jnp.broadcast_to(v_buf[tail0 + k:tail0 + k + 1, :], (n_seq, D_POOL))

    @pl.when((j == last_j) & (b > 0))
    def _():
        mine = lax.broadcasted_iota(jnp.int32, (n_seq, D_POOL), 0) == b
        for k in range(POOL_HIST):
            row = jnp.broadcast_to(v_buf[tail0 + k:tail0 + k + 1, :], (n_seq, D_POOL))
            np_ref[k] = jnp.where(mine, row, np_ref[k])

    cu_buf[0:CONV_PAD, :] = cu_buf[ts:ts + CONV_PAD, :]
    v_buf[0:POOL_PAD, :] = v_buf[ts:ts + POOL_PAD, :]


def _sample_copies(xs_hbm, sc_hbm, sp_hbm, ys_hbm, ncs_hbm, nps_hbm,
                   xs_buf, sc_buf, sp_buf, ncs_buf, nps_buf, in_sems, out_sems):
    n_tok, n_cs = xs_buf.shape[0], sc_buf.shape[0]
    keep = POOL_HIST - n_tok
    ins = ([pltpu.make_async_copy(xs_hbm.at[:, t, :], xs_buf.at[t], in_sems.at[t]) for t in range(n_tok)]
           + [pltpu.make_async_copy(sc_hbm.at[:, k, :], sc_buf.at[k], in_sems.at[n_tok + k])
              for k in range(n_cs)]
           + [pltpu.make_async_copy(sp_hbm, sp_buf, in_sems.at[n_tok + n_cs]),
              pltpu.make_async_copy(sp_hbm.at[pl.ds(n_tok, keep)], nps_hbm.at[pl.ds(0, keep)],
                                    in_sems.at[n_tok + n_cs + 1])])
    outs = ([pltpu.make_async_copy(xs_buf.at[t], ys_hbm.at[:, t, :], out_sems.at[t]) for t in range(n_tok)]
            + [pltpu.make_async_copy(ncs_buf.at[k], ncs_hbm.at[:, k, :], out_sems.at[n_tok + k])
               for k in range(n_cs)]
            + [pltpu.make_async_copy(nps_buf, nps_hbm.at[pl.ds(keep, n_tok)], out_sems.at[n_tok + n_cs])])
    return ins, outs


def _sample_step(weights, xs_buf, sc_buf, sp_buf, ncs_buf, nps_buf, y_prev, hp_buf):
    ng_ref, win_ref, cw_ref, pw_ref, ps_ref, wout_ref, fg_ref = weights
    n_tok, n_seq, _ = xs_buf.shape

    def tok(a, t):
        return a[t * n_seq:(t + 1) * n_seq, :]

    x = jnp.concatenate([xs_buf[t] for t in range(n_tok)], axis=0)
    h = _rmsnorm(x, ng_ref[...])
    _finish_tile(y_prev, hp_buf, fg_ref)
    v = _dot(h, win_ref[:, _COL_V:_COL_V + D_POOL])
    z_b = _dot(h, win_ref[:, _COL_ZB:_COL_ZB + D_POOL])
    proj = _dot(h, win_ref[:, _COL_B:_COL_ZA + D_CONV])

    vp = [sp_buf[k] for k in range(POOL_HIST)] + [tok(v, t) for t in range(n_tok)]
    for t in range(n_tok):
        nps_buf[t] = tok(v, t)
    pooled = []
    for gi, w in enumerate(POOL_WINDOWS):
        lo = gi * POOL_GC
        rows = []
        for t in range(n_tok):
            vg = vp[POOL_HIST + t][:, lo:lo + POOL_GC]
            acc = vg
            for k in range(1, w):
                acc = acc + vp[POOL_HIST + t - k][:, lo:lo + POOL_GC]
            rows.append(acc * (1.0 / w) - vg)
        pooled.append(jnp.concatenate(rows, axis=0))
    y_b = _pool_project(pooled, pw_ref) * ps_ref[...] * _silu(z_b)

    b_g, c_g, u, z_a = (proj[:, c:c + D_CONV] for c in (_COL_B, _COL_C, _COL_U, _COL_ZA))
    cu = c_g * u
    cw = [cw_ref[k] for k in range(CONV_W)]
    up = [sc_buf[k] for k in range(CONV_HIST)] + [tok(cu, t) for t in range(n_tok)]
    for k in range(CONV_HIST):
        ncs_buf[k] = up[n_tok + k]
    yc = []
    for t in range(n_tok):
        acc = cw[0] * up[t]
        for k in range(1, CONV_W):
            acc = acc + cw[k] * up[t + k]
        yc.append(acc)
    y_a = b_g * jnp.concatenate(yc, axis=0) * _silu(z_a)

    ycat = jnp.concatenate([y_a, y_b], axis=-1)
    y = _rmsnorm(x + _dot(ycat, wout_ref[...]), fg_ref[...])
    for t in range(n_tok):
        xs_buf[t] = tok(y, t)


def _mixer_kernel(x_ref, xs_hbm, sc_hbm, sp_hbm, win_hbm, wout_hbm, ng_ref, cw_ref, pw_ref, ps_ref, fg_ref,
                  y_hbm, nc_ref, np_ref, ys_hbm, ncs_hbm, nps_hbm,
                  win_ref, wout_ref, y_stage, hp_buf, cu_buf, v_buf, xs_buf, sc_buf, sp_buf, ncs_buf, nps_buf,
                  w_sems, y_sems, in_sems, out_sems, *, tiles_per_seq):
    i = pl.program_id(0)
    n_prompt = pl.num_programs(0) - 1
    weights = (ng_ref, win_ref, cw_ref, pw_ref, ps_ref, wout_ref, fg_ref)
    w_cols = ((_COL_V, D_POOL), (_COL_ZB, D_POOL)) + tuple((c, 2 * D_CONV) for c in _CONV_PIECE_COLS)
    weight_copies = [pltpu.make_async_copy(win_hbm.at[:, pl.ds(c, n)], win_ref.at[:, pl.ds(c, n)],
                                           w_sems.at[k]) for k, (c, n) in enumerate(w_cols)]
    weight_copies.append(pltpu.make_async_copy(wout_hbm, wout_ref, w_sems.at[len(w_cols)]))
    ins, outs = _sample_copies(xs_hbm, sc_hbm, sp_hbm, ys_hbm, ncs_hbm, nps_hbm,
                               xs_buf, sc_buf, sp_buf, ncs_buf, nps_buf, in_sems, out_sems)

    tail_r0, tail_n = SUB_TILES[-1]

    def y_copy(p, tail):
        slot = p % 2
        r0, n = (tail_r0, tail_n) if tail else (0, tail_r0)
        return pltpu.make_async_copy(
            y_stage.at[slot, pl.ds(r0, n)],
            y_hbm.at[p // tiles_per_seq, pl.ds((p % tiles_per_seq) * SEQ_TILE + r0, n)],
            y_sems.at[2 * slot + int(tail)])

    @pl.when(i >= 2)
    def _():
        y_copy(i - 2, False).wait()

    @pl.when(i >= 3)
    def _():
        y_copy(i - 3, True).wait()

    y_cur, y_prev = y_stage.at[i % 2], y_stage.at[(i + 1) % 2]

    @pl.when(i == 0)
    def _():
        for cp in weight_copies:
            cp.start()
        _prompt_tile(0, 0, tiles_per_seq - 1, x_ref, weights, y_cur, y_prev, hp_buf, nc_ref, np_ref,
                     cu_buf, v_buf, weight_copies)
        y_copy(i, False).start()
        for cp in ins:
            cp.start()

    @pl.when((i > 0) & (i < n_prompt))
    def _():
        _prompt_tile(i // tiles_per_seq, i % tiles_per_seq, tiles_per_seq - 1,
                     x_ref, weights, y_cur, y_prev, hp_buf, nc_ref, np_ref, cu_buf, v_buf)
        y_copy(i, False).start()
        y_copy(i - 1, True).start()

    @pl.when(i == n_prompt)
    def _():
        for cp in ins:
            cp.wait()
        _sample_step(weights, xs_buf, sc_buf, sp_buf, ncs_buf, nps_buf, y_prev, hp_buf)
        y_copy(i - 1, True).start()
        for cp in outs:
            cp.start()
        for cp in outs:
            cp.wait()
        y_copy(i - 1, False).wait()
        y_copy(i - 2, True).wait()
        y_copy(i - 1, True).wait()


def _const_spec(shape):
    nd = len(shape)
    return pl.BlockSpec(shape, lambda *_: (0,) * nd, pipeline_mode=pl.Buffered(1))


def kernel(x_prompt, x_sample, state_conv, state_pool, norm_g, w_in, conv_w, pool_w, pool_scale,
           w_out, final_g):
    assert norm_g.shape[0] == 1, "single-layer trunk"
    batch, seq, _ = x_prompt.shape
    n_seq, n_tok, _ = x_sample.shape
    assert seq % SEQ_TILE == 0 and n_tok < POOL_HIST and CONV_HIST <= CONV_PAD
    assert SUB_TILES[0][0] == 0 and sum(n for _, n in SUB_TILES) == SEQ_TILE
    assert all(a + n == b for (a, n), (b, _) in zip(SUB_TILES, SUB_TILES[1:]))
    assert all(n % BF16_TILE_ROWS == 0 and n >= POOL_MAX for _, n in SUB_TILES)
    tiles_per_seq = seq // SEQ_TILE
    n_prompt = batch * tiles_per_seq
    assert n_prompt >= 2, "the output copy schedule drains two tiles in the decode step"

    small = (norm_g, jnp.swapaxes(conv_w, 0, 1), pool_w[0], pool_scale, final_g.reshape(1, D_MODEL))

    def tile(i):
        p = jnp.minimum(i, n_prompt - 1)
        return p // tiles_per_seq, p % tiles_per_seq

    hbm = pl.BlockSpec(memory_space=pl.ANY)
    outs = pl.pallas_call(
        functools.partial(_mixer_kernel, tiles_per_seq=tiles_per_seq),
        grid=(n_prompt + 1,),
        in_specs=[pl.BlockSpec((1, SEQ_TILE, D_MODEL), lambda i: (*tile(i), 0))] + [hbm] * 5
        + [_const_spec(a.shape) for a in small],
        out_specs=[
            hbm,
            pl.BlockSpec((1, batch, CONV_HIST, D_CONV), lambda i: (0, 0, 0, 0)),
            pl.BlockSpec((POOL_HIST, batch, D_POOL), lambda i: (0, 0, 0)),
            hbm, hbm, hbm,
        ],
        out_shape=[
            jax.ShapeDtypeStruct((batch, seq, D_MODEL), _F32),
            jax.ShapeDtypeStruct((1, batch, CONV_HIST, D_CONV), _F32),
            jax.ShapeDtypeStruct((POOL_HIST, batch, D_POOL), _F32),
            jax.ShapeDtypeStruct((n_seq, n_tok, D_MODEL), _F32),
            jax.ShapeDtypeStruct((n_seq, CONV_HIST, D_CONV), _F32),
            jax.ShapeDtypeStruct((POOL_HIST, n_seq, D_POOL), _F32),
        ],
        scratch_shapes=[
            pltpu.VMEM((D_MODEL, D_IN), _F32),
            pltpu.VMEM((D_MODEL, D_MODEL), _F32),
            pltpu.VMEM((2, SEQ_TILE, D_MODEL), _F32),
            pltpu.VMEM((SUB_TILES[-1][1], D_MODEL), _F32),
            pltpu.VMEM((CONV_PAD + SEQ_TILE, D_CONV), _F32),
            pltpu.VMEM((POOL_PAD + SEQ_TILE, D_POOL), _F32),
            pltpu.VMEM((n_tok, n_seq, D_MODEL), _F32),
            pltpu.VMEM((CONV_HIST, n_seq, D_CONV), _F32),
            pltpu.VMEM((POOL_HIST, n_seq, D_POOL), _F32),
            pltpu.VMEM((CONV_HIST, n_seq, D_CONV), _F32),
            pltpu.VMEM((n_tok, n_seq, D_POOL), _F32),
            pltpu.SemaphoreType.DMA((3 + len(_CONV_PIECE_COLS),)),
            pltpu.SemaphoreType.DMA((4,)),
            pltpu.SemaphoreType.DMA((n_tok + CONV_HIST + 2,)),
            pltpu.SemaphoreType.DMA((n_tok + CONV_HIST + 1,)),
        ],
        compiler_params=pltpu.CompilerParams(
            dimension_semantics=("arbitrary",),
            vmem_limit_bytes=VMEM_LIMIT_BYTES),
        name="mixer_step",
    )(x_prompt, x_sample, state_conv[0], jnp.swapaxes(state_pool[0], 0, 1), w_in[0], w_out[0], *small)
    y_prompt, nc_p, np_p, y_sample, nc_s, np_s = outs
    return (y_prompt, y_sample, nc_p, jnp.swapaxes(np_p, 0, 1)[None], nc_s[None],
            jnp.swapaxes(np_s, 0, 1)[None])
```

```python
import functools

import jax
import jax.numpy as jnp
from jax import lax
from jax.experimental import pallas as pl
from jax.experimental.pallas import tpu as pltpu

D_MODEL = 1024
D_CONV = 512
D_POOL = 512
POOL_WINDOWS = (2, 4, 8, 16)
POOL_GC = D_POOL // len(POOL_WINDOWS)
POOL_MAX = max(POOL_WINDOWS)
POOL_HIST = POOL_MAX - 1
CONV_W = 3
CONV_HIST = CONV_W - 1
D_IN = 4 * D_CONV + 2 * D_POOL
EPS = 1e-6

SEQ_TILE = 1024
SUB_TILES = ((0, 384), (384, 384), (768, 256))
BF16_TILE_ROWS = 16
CONV_PAD = 8
POOL_PAD = 16
VMEM_LIMIT_BYTES = 60 * 1024 * 1024

_F32 = jnp.float32

_COL_B, _COL_C, _COL_U, _COL_ZA = 0, D_CONV, 2 * D_CONV, 3 * D_CONV
_COL_V, _COL_ZB = 4 * D_CONV, 4 * D_CONV + D_POOL
_CONV_PIECE_COLS = (_COL_B, _COL_U)


def _rmsnorm(x, g):
    ms = jnp.mean(x * x, axis=-1, keepdims=True)
    return (x * lax.rsqrt(ms + EPS)) * g


def _silu(z):
    return z * jax.nn.sigmoid(z)


def _dot(a, b):
    return jnp.dot(a.astype(jnp.bfloat16), b.astype(jnp.bfloat16), preferred_element_type=_F32)


def _pool_project(pooled_groups, pw_ref):
    outs = [_dot(p, pw_ref[gi]) for gi, p in enumerate(pooled_groups)]
    return jnp.concatenate(outs, axis=-1)


def _window_sums(ext):
    assert all(b == 2 * a for a, b in zip((1,) + POOL_WINDOWS, POOL_WINDOWS)) and POOL_PAD >= POOL_MAX
    sums, width, out = ext, 1, []
    for w in POOL_WINDOWS:
        while width < w:
            sums = sums + pltpu.roll(sums, width, axis=0)
            width *= 2
        out.append(sums[POOL_PAD:, 0:POOL_GC])
        sums = sums[:, POOL_GC:]
    return out


def _finish_tile(y_stage, hp_buf, fg_ref):
    r0, n = SUB_TILES[-1]
    y_stage[r0:r0 + n, :] = _rmsnorm(hp_buf[...], fg_ref[...])


def _prompt_tile(b, j, last_j, x_ref, weights, y_cur, y_prev, hp_buf, nc_ref, np_ref, cu_buf, v_buf,
                 weight_copies=None):
    def arrive(k):
        if weight_copies is not None:
            weight_copies[k].wait()

    ng_ref, win_ref, cw_ref, pw_ref, ps_ref, wout_ref, fg_ref = weights
    ts = SEQ_TILE
    head = POOL_MAX

    @pl.when(j == 0)
    def _():
        cu_buf[0:CONV_PAD, :] = jnp.zeros((CONV_PAD, D_CONV), _F32)
        v_buf[0:POOL_PAD, :] = jnp.zeros((POOL_PAD, D_POOL), _F32)

    cw = [cw_ref[k] for k in range(CONV_W)]
    pos = j * ts + 1 + lax.broadcasted_iota(jnp.int32, (head, POOL_GC), 0)

    xs = [x_ref[0, r0:r0 + n, :] for r0, n in SUB_TILES]
    hs = [_rmsnorm(x, ng_ref[...]) for x in xs]
    if weight_copies is None:
        _finish_tile(y_prev, hp_buf, fg_ref)
    arrive(0)
    vs = [_dot(h, win_ref[:, _COL_V:_COL_V + D_POOL]) for h in hs]
    arrive(1)
    zbs = [_dot(h, win_ref[:, _COL_ZB:_COL_ZB + D_POOL]) for h in hs]

    def conv_projections():
        if weight_copies is None:
            projs = [_dot(h, win_ref[:, _COL_B:_COL_ZA + D_CONV]) for h in hs]
            return [[proj[:, c:c + D_CONV] for proj in projs] for c in (_COL_B, _COL_C, _COL_U, _COL_ZA)]
        parts = []
        for k, col in enumerate(_CONV_PIECE_COLS):
            arrive(2 + k)
            pair = [_dot(h, win_ref[:, col:col + 2 * D_CONV]) for h in hs]
            parts += [[p[:, 0:D_CONV] for p in pair], [p[:, D_CONV:] for p in pair]]
        return parts

    def pool_means():
        pooled = []
        for s, v in enumerate(vs):
            r0, n = SUB_TILES[s]
            p0 = POOL_PAD + r0
            v_buf[p0:p0 + n, :] = v
            wsums = _window_sums(v_buf[p0 - POOL_PAD:p0 + n, :])
            groups = []
            for gi, w in enumerate(POOL_WINDOWS):
                acc = wsums[gi]
                if s == 0:
                    cnt = jnp.minimum(pos, w).astype(_F32)
                    mean = jnp.concatenate([acc[0:head] / cnt, acc[head:] * (1.0 / w)], axis=0)
                else:
                    mean = acc * (1.0 / w)
                groups.append(mean - v[:, gi * POOL_GC:(gi + 1) * POOL_GC])
            pooled.append(groups)
        return pooled

    def pool_outputs(pooled):
        return [_pool_project(p, pw_ref) * ps_ref[...] * _silu(z_b) for p, z_b in zip(pooled, zbs)]

    if weight_copies is None:
        b_gs, c_gs, us, z_as = conv_projections()
        pooled = pool_means()
    else:
        y_bs = pool_outputs(pool_means())
        b_gs, c_gs, us, z_as = conv_projections()

    y_as = []
    for s, (r0, n) in enumerate(SUB_TILES):
        cu = c_gs[s] * us[s]
        c0 = CONV_PAD + r0
        cu_buf[c0:c0 + n, :] = cu
        ext = cu_buf[c0 - CONV_PAD:c0 + n, :]
        yc = cw[0] * pltpu.roll(ext, CONV_HIST, axis=0)[CONV_PAD:]
        for k in range(1, CONV_HIST):
            yc = yc + cw[k] * pltpu.roll(ext, CONV_HIST - k, axis=0)[CONV_PAD:]
        yc = yc + cw[CONV_HIST] * cu
        y_as.append(b_gs[s] * yc * _silu(z_as[s]))

    if weight_copies is None:
        y_bs = pool_outputs(pooled)
    arrive(2 + len(_CONV_PIECE_COLS))
    mixes = [_dot(jnp.concatenate([y_a, y_b], axis=-1), wout_ref[...]) for y_a, y_b in zip(y_as, y_bs)]
    for (r0, n), x, mix in zip(SUB_TILES[:-1], xs, mixes):
        y_cur[r0:r0 + n, :] = _rmsnorm(x + mix, fg_ref[...])
    hp_buf[...] = xs[-1] + mixes[-1]

    @pl.when(j == last_j)
    def _():
        nc_ref[0, b] = cu_buf[CONV_PAD + ts - CONV_HIST:CONV_PAD + ts, :]

    tail0 = POOL_PAD + ts - POOL_HIST
    n_seq = np_ref.shape[1]

    @pl.when((j == last_j) & (b == 0))
    def _():
        for k in range(POOL_HIST):
            np_ref[k] = jnp.broadcast_to(v_buf[tail0 + k:tail0 + k + 1, :], (n_seq, D_POOL))

    @pl.when((j == last_j) & (b > 0))
    def _():
        mine = lax.broadcasted_iota(jnp.int32, (n_seq, D_POOL), 0) == b
        for k in range(POOL_HIST):
            row = jnp.broadcast_to(v_buf[tail0 + k:tail0 + k + 1, :], (n_seq, D_POOL))
            np_ref[k] = jnp.where(mine, row, np_ref[k])

    cu_buf[0:CONV_PAD, :] = cu_buf[ts:ts + CONV_PAD, :]
    v_buf[0:POOL_PAD, :] = v_buf[ts:ts + POOL_PAD, :]


def _sample_copies(xs_hbm, sc_hbm, sp_hbm, ys_hbm, ncs_hbm, nps_hbm,
                   xs_buf, sc_buf, sp_buf, ncs_buf, nps_buf, in_sems, out_sems):
    n_tok, n_cs = xs_buf.shape[0], sc_buf.shape[0]
    keep = POOL_HIST - n_tok
    ins = ([pltpu.make_async_copy(xs_hbm.at[:, t, :], xs_buf.at[t], in_sems.at[t]) for t in range(n_tok)]
           + [pltpu.make_async_copy(sc_hbm.at[:, k, :], sc_buf.at[k], in_sems.at[n_tok + k])
              for k in range(n_cs)]
           + [pltpu.make_async_copy(sp_hbm, sp_buf, in_sems.at[n_tok + n_cs]),
              pltpu.make_async_copy(sp_hbm.at[pl.ds(n_tok, keep)], nps_hbm.at[pl.ds(0, keep)],
                                    in_sems.at[n_tok + n_cs + 1])])
    outs = ([pltpu.make_async_copy(xs_buf.at[t], ys_hbm.at[:, t, :], out_sems.at[t]) for t in range(n_tok)]
            + [pltpu.make_async_copy(ncs_buf.at[k], ncs_hbm.at[:, k, :], out_sems.at[n_tok + k])
               for k in range(n_cs)]
            + [pltpu.make_async_copy(nps_buf, nps_hbm.at[pl.ds(keep, n_tok)], out_sems.at[n_tok + n_cs])])
    return ins, outs


def _sample_step(weights, xs_buf, sc_buf, sp_buf, ncs_buf, nps_buf, y_prev, hp_buf):
    ng_ref, win_ref, cw_ref, pw_ref, ps_ref, wout_ref, fg_ref = weights
    n_tok, n_seq, _ = xs_buf.shape

    def tok(a, t):
        return a[t * n_seq:(t + 1) * n_seq, :]

    x = jnp.concatenate([xs_buf[t] for t in range(n_tok)], axis=0)
    h = _rmsnorm(x, ng_ref[...])
    _finish_tile(y_prev, hp_buf, fg_ref)
    v = _dot(h, win_ref[:, _COL_V:_COL_V + D_POOL])
    z_b = _dot(h, win_ref[:, _COL_ZB:_COL_ZB + D_POOL])
    proj = _dot(h, win_ref[:, _COL_B:_COL_ZA + D_CONV])

    vp = [sp_buf[k] for k in range(POOL_HIST)] + [tok(v, t) for t in range(n_tok)]
    for t in range(n_tok):
        nps_buf[t] = tok(v, t)
    pooled = []
    for gi, w in enumerate(POOL_WINDOWS):
        lo = gi * POOL_GC
        rows = []
        for t in range(n_tok):
            vg = vp[POOL_HIST + t][:, lo:lo + POOL_GC]
            acc = vg
            for k in range(1, w):
                acc = acc + vp[POOL_HIST + t - k][:, lo:lo + POOL_GC]
            rows.append(acc * (1.0 / w) - vg)
        pooled.append(jnp.concatenate(rows, axis=0))
    y_b = _pool_project(pooled, pw_ref) * ps_ref[...] * _silu(z_b)

    b_g, c_g, u, z_a = (proj[:, c:c + D_CONV] for c in (_COL_B, _COL_C, _COL_U, _COL_ZA))
    cu = c_g * u
    cw = [cw_ref[k] for k in range(CONV_W)]
    up = [sc_buf[k] for k in range(CONV_HIST)] + [tok(cu, t) for t in range(n_tok)]
    for k in range(CONV_HIST):
        ncs_buf[k] = up[n_tok + k]
    yc = []
    for t in range(n_tok):
        acc = cw[0] * up[t]
        for k in range(1, CONV_W):
            acc = acc + cw[k] * up[t + k]
        yc.append(acc)
    y_a = b_g * jnp.concatenate(yc, axis=0) * _silu(z_a)

    ycat = jnp.concatenate([y_a, y_b], axis=-1)
    y = _rmsnorm(x + _dot(ycat, wout_ref[...]), fg_ref[...])
    for t in range(n_tok):
        xs_buf[t] = tok(y, t)


def _mixer_kernel(x_ref, xs_hbm, sc_hbm, sp_hbm, win_hbm, wout_hbm, ng_ref, cw_ref, pw_ref, ps_ref, fg_ref,
                  y_hbm, nc_ref, np_ref, ys_hbm, ncs_hbm, nps_hbm,
                  win_ref, wout_ref, y_stage, hp_buf, cu_buf, v_buf, xs_buf, sc_buf, sp_buf, ncs_buf, nps_buf,
                  w_sems, y_sems, in_sems, out_sems, *, tiles_per_seq):
    i = pl.program_id(0)
    n_prompt = pl.num_programs(0) - 1
    weights = (ng_ref, win_ref, cw_ref, pw_ref, ps_ref, wout_ref, fg_ref)
    w_cols = ((_COL_V, D_POOL), (_COL_ZB, D_POOL)) + tuple((c, 2 * D_CONV) for c in _CONV_PIECE_COLS)
    weight_copies = [pltpu.make_async_copy(win_hbm.at[:, pl.ds(c, n)], win_ref.at[:, pl.ds(c, n)],
                                           w_sems.at[k]) for k, (c, n) in enumerate(w_cols)]
    weight_copies.append(pltpu.make_async_copy(wout_hbm, wout_ref, w_sems.at[len(w_cols)]))
    ins, outs = _sample_copies(xs_hbm, sc_hbm, sp_hbm, ys_hbm, ncs_hbm, nps_hbm,
                               xs_buf, sc_buf, sp_buf, ncs_buf, nps_buf, in_sems, out_sems)

    tail_r0, tail_n = SUB_TILES[-1]

    def y_copy(p, tail):
        slot = p % 2
        r0, n = (tail_r0, tail_n) if tail else (0, tail_r0)
        return pltpu.make_async_copy(
            y_stage.at[slot, pl.ds(r0, n)],
            y_hbm.at[p // tiles_per_seq, pl.ds((p % tiles_per_seq) * SEQ_TILE + r0, n)],
            y_sems.at[2 * slot + int(tail)])

    @pl.when(i >= 2)
    def _():
        y_copy(i - 2, False).wait()

    @pl.when(i >= 3)
    def _():
        y_copy(i - 3, True).wait()

    y_cur, y_prev = y_stage.at[i % 2], y_stage.at[(i + 1) % 2]

    @pl.when(i == 0)
    def _():
        for cp in weight_copies:
            cp.start()
        _prompt_tile(0, 0, tiles_per_seq - 1, x_ref, weights, y_cur, y_prev, hp_buf, nc_ref, np_ref,
                     cu_buf, v_buf, weight_copies)
        y_copy(i, False).start()
        for cp in ins:
            cp.start()

    @pl.when((i > 0) & (i < n_prompt))
    def _():
        _prompt_tile(i // tiles_per_seq, i % tiles_per_seq, tiles_per_seq - 1,
                     x_ref, weights, y_cur, y_prev, hp_buf, nc_ref, np_ref, cu_buf, v_buf)
        y_copy(i, False).start()
        y_copy(i - 1, True).start()

    @pl.when(i == n_prompt)
    def _():
        for cp in ins:
            cp.wait()
        _sample_step(weights, xs_buf, sc_buf, sp_buf, ncs_buf, nps_buf, y_prev, hp_buf)
        y_copy(i - 1, True).start()
        for cp in outs:
            cp.start()
        for cp in outs:
            cp.wait()
        y_copy(i - 1, False).wait()
        y_copy(i - 2, True).wait()
        y_copy(i - 1, True).wait()


def _const_spec(shape):
    nd = len(shape)
    return pl.BlockSpec(shape, lambda *_: (0,) * nd, pipeline_mode=pl.Buffered(1))


def kernel(x_prompt, x_sample, state_conv, state_pool, norm_g, w_in, conv_w, pool_w, pool_scale,
           w_out, final_g):
    assert norm_g.shape[0] == 1, "single-layer trunk"
    batch, seq, _ = x_prompt.shape
    n_seq, n_tok, _ = x_sample.shape
    assert seq % SEQ_TILE == 0 and n_tok < POOL_HIST and CONV_HIST <= CONV_PAD
    assert SUB_TILES[0][0] == 0 and sum(n for _, n in SUB_TILES) == SEQ_TILE
    assert all(a + n == b for (a, n), (b, _) in zip(SUB_TILES, SUB_TILES[1:]))
    assert all(n % BF16_TILE_ROWS == 0 and n >= POOL_MAX for _, n in SUB_TILES)
    tiles_per_seq = seq // SEQ_TILE
    n_prompt = batch * tiles_per_seq
    assert n_prompt >= 2, "the output copy schedule drains two tiles in the decode step"

    small = (norm_g, jnp.swapaxes(conv_w, 0, 1), pool_w[0], pool_scale, final_g.reshape(1, D_MODEL))

    def tile(i):
        p = jnp.minimum(i, n_prompt - 1)
        return p // tiles_per_seq, p % tiles_per_seq

    hbm = pl.BlockSpec(memory_space=pl.ANY)
    outs = pl.pallas_call(
        functools.partial(_mixer_kernel, tiles_per_seq=tiles_per_seq),
        grid=(n_prompt + 1,),
        in_specs=[pl.BlockSpec((1, SEQ_TILE, D_MODEL), lambda i: (*tile(i), 0))] + [hbm] * 5
        + [_const_spec(a.shape) for a in small],
        out_specs=[
            hbm,
            pl.BlockSpec((1, batch, CONV_HIST, D_CONV), lambda i: (0, 0, 0, 0)),
            pl.BlockSpec((POOL_HIST, batch, D_POOL), lambda i: (0, 0, 0)),
            hbm, hbm, hbm,
        ],
        out_shape=[
            jax.ShapeDtypeStruct((batch, seq, D_MODEL), _F32),
            jax.ShapeDtypeStruct((1, batch, CONV_HIST, D_CONV), _F32),
            jax.ShapeDtypeStruct((POOL_HIST, batch, D_POOL), _F32),
            jax.ShapeDtypeStruct((n_seq, n_tok, D_MODEL), _F32),
            jax.ShapeDtypeStruct((n_seq, CONV_HIST, D_CONV), _F32),
            jax.ShapeDtypeStruct((POOL_HIST, n_seq, D_POOL), _F32),
        ],
        scratch_shapes=[
            pltpu.VMEM((D_MODEL, D_IN), _F32),
            pltpu.VMEM((D_MODEL, D_MODEL), _F32),
            pltpu.VMEM((2, SEQ_TILE, D_MODEL), _F32),
            pltpu.VMEM((SUB_TILES[-1][1], D_MODEL), _F32),
            pltpu.VMEM((CONV_PAD + SEQ_TILE, D_CONV), _F32),
            pltpu.VMEM((POOL_PAD + SEQ_TILE, D_POOL), _F32),
            pltpu.VMEM((n_tok, n_seq, D_MODEL), _F32),
            pltpu.VMEM((CONV_HIST, n_seq, D_CONV), _F32),
            pltpu.VMEM((POOL_HIST, n_seq, D_POOL), _F32),
            pltpu.VMEM((CONV_HIST, n_seq, D_CONV), _F32),
            pltpu.VMEM((n_tok, n_seq, D_POOL), _F32),
            pltpu.SemaphoreType.DMA((3 + len(_CONV_PIECE_COLS),)),
            pltpu.SemaphoreType.DMA((4,)),
            pltpu.SemaphoreType.DMA((n_tok + CONV_HIST + 2,)),
            pltpu.SemaphoreType.DMA((n_tok + CONV_HIST + 1,)),
        ],
        compiler_params=pltpu.CompilerParams(
            dimension_semantics=("arbitrary",),
            vmem_limit_bytes=VMEM_LIMIT_BYTES),
        name="mixer_step",
    )(x_prompt, x_sample, state_conv[0], jnp.swapaxes(state_pool[0], 0, 1), w_in[0], w_out[0], *small)
    y_prompt, nc_p, np_p, y_sample, nc_s, np_s = outs
    return (y_prompt, y_sample, nc_p, jnp.swapaxes(np_p, 0, 1)[None], nc_s[None],
            jnp.swapaxes(np_s, 0, 1)[None])
```

```python
import functools

import jax
import jax.numpy as jnp
from jax import lax
from jax.experimental import pallas as pl
from jax.experimental.pallas import tpu as pltpu

D_MODEL = 1024
D_CONV = 512
D_POOL = 512
POOL_WINDOWS = (2, 4, 8, 16)
POOL_GC = D_POOL // len(POOL_WINDOWS)
POOL_MAX = max(POOL_WINDOWS)
POOL_HIST = POOL_MAX - 1
CONV_W = 3
CONV_HIST = CONV_W - 1
D_IN = 4 * D_CONV + 2 * D_POOL
EPS = 1e-6

SEQ_TILE = 1024
SUB_TILES = ((0, 384), (384, 384), (768, 256))
BF16_TILE_ROWS = 16
CONV_PAD = 8
POOL_PAD = 16
VMEM_LIMIT_BYTES = 60 * 1024 * 1024

_F32 = jnp.float32

_COL_B, _COL_C, _COL_U, _COL_ZA = 0, D_CONV, 2 * D_CONV, 3 * D_CONV
_COL_V, _COL_ZB = 4 * D_CONV, 4 * D_CONV + D_POOL


def _rmsnorm(x, g):
    ms = jnp.mean(x * x, axis=-1, keepdims=True)
    return (x * lax.rsqrt(ms + EPS)) * g


def _silu(z):
    return z * jax.nn.sigmoid(z)


def _dot(a, b):
    return jnp.dot(a.astype(jnp.bfloat16), b.astype(jnp.bfloat16), preferred_element_type=_F32)


def _pool_project(pooled_groups, pw_ref):
    outs = [_dot(p, pw_ref[gi]) for gi, p in enumerate(pooled_groups)]
    return jnp.concatenate(outs, axis=-1)


def _window_sums(ext):
    assert all(b == 2 * a for a, b in zip((1,) + POOL_WINDOWS, POOL_WINDOWS)) and POOL_PAD >= POOL_MAX
    sums, width, out = ext, 1, []
    for w in POOL_WINDOWS:
        while width < w:
            sums = sums + pltpu.roll(sums, width, axis=0)
            width *= 2
        out.append(sums[POOL_PAD:, 0:POOL_GC])
        sums = sums[:, POOL_GC:]
    return out


def _finish_tile(y_stage, hp_buf, fg_ref):
    r0, n = SUB_TILES[-1]
    y_stage[r0:r0 + n, :] = _rmsnorm(hp_buf[...], fg_ref[...])


def _prompt_tile(b, j, last_j, x_ref, weights, y_cur, y_prev, hp_buf, nc_ref, np_ref, cu_buf, v_buf,
                 weight_copies=None):
    def arrive(k):
        if weight_copies is not None:
            weight_copies[k].wait()

    ng_ref, win_ref, cw_ref, pw_ref, ps_ref, wout_ref, fg_ref = weights
    ts = SEQ_TILE
    head = POOL_MAX

    @pl.when(j == 0)
    def _():
        cu_buf[0:CONV_PAD, :] = jnp.zeros((CONV_PAD, D_CONV), _F32)
        v_buf[0:POOL_PAD, :] = jnp.zeros((POOL_PAD, D_POOL), _F32)

    cw = [cw_ref[k] for k in range(CONV_W)]
    pos = j * ts + 1 + lax.broadcasted_iota(jnp.int32, (head, POOL_GC), 0)

    xs = [x_ref[0, r0:r0 + n, :] for r0, n in SUB_TILES]
    hs = [_rmsnorm(x, ng_ref[...]) for x in xs]
    if weight_copies is None:
        _finish_tile(y_prev, hp_buf, fg_ref)
    arrive(0)
    vs = [_dot(h, win_ref[:, _COL_V:_COL_V + D_POOL]) for h in hs]
    arrive(1)
    zbs = [_dot(h, win_ref[:, _COL_ZB:_COL_ZB + D_POOL]) for h in hs]
    arrive(2)
    projs = [_dot(h, win_ref[:, _COL_B:_COL_ZA + D_CONV]) for h in hs]
    b_gs, c_gs, us, z_as = ([proj[:, c:c + D_CONV] for proj in projs]
                            for c in (_COL_B, _COL_C, _COL_U, _COL_ZA))

    pooled = []
    for s, v in enumerate(vs):
        r0, n = SUB_TILES[s]
        p0 = POOL_PAD + r0
        v_buf[p0:p0 + n, :] = v
        wsums = _window_sums(v_buf[p0 - POOL_PAD:p0 + n, :])
        groups = []
        for gi, w in enumerate(POOL_WINDOWS):
            acc = wsums[gi]
            if s == 0:
                cnt = jnp.minimum(pos, w).astype(_F32)
                mean = jnp.concatenate([acc[0:head] / cnt, acc[head:] * (1.0 / w)], axis=0)
            else:
                mean = acc * (1.0 / w)
            groups.append(mean - v[:, gi * POOL_GC:(gi + 1) * POOL_GC])
        pooled.append(groups)

    y_as = []
    for s, (r0, n) in enumerate(SUB_TILES):
        cu = c_gs[s] * us[s]
        c0 = CONV_PAD + r0
        cu_buf[c0:c0 + n, :] = cu
        ext = cu_buf[c0 - CONV_PAD:c0 + n, :]
        yc = cw[0] * pltpu.roll(ext, CONV_HIST, axis=0)[CONV_PAD:]
        for k in range(1, CONV_HIST):
            yc = yc + cw[k] * pltpu.roll(ext, CONV_HIST - k, axis=0)[CONV_PAD:]
        yc = yc + cw[CONV_HIST] * cu
        y_as.append(b_gs[s] * yc * _silu(z_as[s]))

    y_bs = [_pool_project(p, pw_ref) * ps_ref[...] * _silu(z_b) for p, z_b in zip(pooled, zbs)]
    arrive(3)
    mixes = [_dot(jnp.concatenate([y_a, y_b], axis=-1), wout_ref[...]) for y_a, y_b in zip(y_as, y_bs)]
    for (r0, n), x, mix in zip(SUB_TILES[:-1], xs, mixes):
        y_cur[r0:r0 + n, :] = _rmsnorm(x + mix, fg_ref[...])
    hp_buf[...] = xs[-1] + mixes[-1]

    @pl.when(j == last_j)
    def _():
        nc_ref[0, b] = cu_buf[CONV_PAD + ts - CONV_HIST:CONV_PAD + ts, :]

    tail0 = POOL_PAD + ts - POOL_HIST
    n_seq = np_ref.shape[1]

    @pl.when((j == last_j) & (b == 0))
    def _():
        for k in range(POOL_HIST):
            np_ref[k] = jnp.broadcast_to(v_buf[tail0 + k:tail0 + k + 1, :], (n_seq, D_POOL))

    @pl.when((j == last_j) & (b > 0))
    def _():
        mine = lax.broadcasted_iota(jnp.int32, (n_seq, D_POOL), 0) == b
        for k in range(POOL_HIST):
            row = jnp.broadcast_to(v_buf[tail0 + k:tail0 + k + 1, :], (n_seq, D_POOL))
            np_ref[k] = jnp.where(mine, row, np_ref[k])

    cu_buf[0:CONV_PAD, :] = cu_buf[ts:ts + CONV_PAD, :]
    v_buf[0:POOL_PAD, :] = v_buf[ts:ts + POOL_PAD, :]


def _sample_copies(xs_hbm, sc_hbm, sp_hbm, ys_hbm, ncs_hbm, nps_hbm,
                   xs_buf, sc_buf, sp_buf, ncs_buf, nps_buf, in_sems, out_sems):
    n_tok, n_cs = xs_buf.shape[0], sc_buf.shape[0]
    keep = POOL_HIST - n_tok
    ins = ([pltpu.make_async_copy(xs_hbm.at[:, t, :], xs_buf.at[t], in_sems.at[t]) for t in range(n_tok)]
           + [pltpu.make_async_copy(sc_hbm.at[:, k, :], sc_buf.at[k], in_sems.at[n_tok + k])
              for k in range(n_cs)]
           + [pltpu.make_async_copy(sp_hbm, sp_buf, in_sems.at[n_tok + n_cs]),
              pltpu.make_async_copy(sp_hbm.at[pl.ds(n_tok, keep)], nps_hbm.at[pl.ds(0, keep)],
                                    in_sems.at[n_tok + n_cs + 1])])
    outs = ([pltpu.make_async_copy(xs_buf.at[t], ys_hbm.at[:, t, :], out_sems.at[t]) for t in range(n_tok)]
            + [pltpu.make_async_copy(ncs_buf.at[k], ncs_hbm.at[:, k, :], out_sems.at[n_tok + k])
               for k in range(n_cs)]
            + [pltpu.make_async_copy(nps_buf, nps_hbm.at[pl.ds(keep, n_tok)], out_sems.at[n_tok + n_cs])])
    return ins, outs


def _sample_step(weights, xs_buf, sc_buf, sp_buf, ncs_buf, nps_buf, y_prev, hp_buf):
    ng_ref, win_ref, cw_ref, pw_ref, ps_ref, wout_ref, fg_ref = weights
    n_tok, n_seq, _ = xs_buf.shape

    def tok(a, t):
        return a[t * n_seq:(t + 1) * n_seq, :]

    x = jnp.concatenate([xs_buf[t] for t in range(n_tok)], axis=0)
    h = _rmsnorm(x, ng_ref[...])
    _finish_tile(y_prev, hp_buf, fg_ref)
    v = _dot(h, win_ref[:, _COL_V:_COL_V + D_POOL])
    z_b = _dot(h, win_ref[:, _COL_ZB:_COL_ZB + D_POOL])
    proj = _dot(h, win_ref[:, _COL_B:_COL_ZA + D_CONV])

    vp = [sp_buf[k] for k in range(POOL_HIST)] + [tok(v, t) for t in range(n_tok)]
    for t in range(n_tok):
        nps_buf[t] = tok(v, t)
    pooled = []
    for gi, w in enumerate(POOL_WINDOWS):
        lo = gi * POOL_GC
        rows = []
        for t in range(n_tok):
            vg = vp[POOL_HIST + t][:, lo:lo + POOL_GC]
            acc = vg
            for k in range(1, w):
                acc = acc + vp[POOL_HIST + t - k][:, lo:lo + POOL_GC]
            rows.append(acc * (1.0 / w) - vg)
        pooled.append(jnp.concatenate(rows, axis=0))
    y_b = _pool_project(pooled, pw_ref) * ps_ref[...] * _silu(z_b)

    b_g, c_g, u, z_a = (proj[:, c:c + D_CONV] for c in (_COL_B, _COL_C, _COL_U, _COL_ZA))
    cu = c_g * u
    cw = [cw_ref[k] for k in range(CONV_W)]
    up = [sc_buf[k] for k in range(CONV_HIST)] + [tok(cu, t) for t in range(n_tok)]
    for k in range(CONV_HIST):
        ncs_buf[k] = up[n_tok + k]
    yc = []
    for t in range(n_tok):
        acc = cw[0] * up[t]
        for k in range(1, CONV_W):
            acc = acc + cw[k] * up[t + k]
        yc.append(acc)
    y_a = b_g * jnp.concatenate(yc, axis=0) * _silu(z_a)

    ycat = jnp.concatenate([y_a, y_b], axis=-1)
    y = _rmsnorm(x + _dot(ycat, wout_ref[...]), fg_ref[...])
    for t in range(n_tok):
        xs_buf[t] = tok(y, t)


def _mixer_kernel(x_ref, xs_hbm, sc_hbm, sp_hbm, win_hbm, wout_hbm, ng_ref, cw_ref, pw_ref, ps_ref, fg_ref,
                  y_hbm, nc_ref, np_ref, ys_hbm, ncs_hbm, nps_hbm,
                  win_ref, wout_ref, y_stage, hp_buf, cu_buf, v_buf, xs_buf, sc_buf, sp_buf, ncs_buf, nps_buf,
                  w_sems, y_sems, in_sems, out_sems, *, tiles_per_seq):
    i = pl.program_id(0)
    n_prompt = pl.num_programs(0) - 1
    weights = (ng_ref, win_ref, cw_ref, pw_ref, ps_ref, wout_ref, fg_ref)
    w_cols = ((_COL_V, D_POOL), (_COL_ZB, D_POOL), (_COL_B, 4 * D_CONV))
    weight_copies = [pltpu.make_async_copy(win_hbm.at[:, pl.ds(c, n)], win_ref.at[:, pl.ds(c, n)],
                                           w_sems.at[k]) for k, (c, n) in enumerate(w_cols)]
    weight_copies.append(pltpu.make_async_copy(wout_hbm, wout_ref, w_sems.at[len(w_cols)]))
    ins, outs = _sample_copies(xs_hbm, sc_hbm, sp_hbm, ys_hbm, ncs_hbm, nps_hbm,
                               xs_buf, sc_buf, sp_buf, ncs_buf, nps_buf, in_sems, out_sems)

    tail_r0, tail_n = SUB_TILES[-1]

    def y_copy(p, tail):
        slot = p % 2
        r0, n = (tail_r0, tail_n) if tail else (0, tail_r0)
        return pltpu.make_async_copy(
            y_stage.at[slot, pl.ds(r0, n)],
            y_hbm.at[p // tiles_per_seq, pl.ds((p % tiles_per_seq) * SEQ_TILE + r0, n)],
            y_sems.at[2 * slot + int(tail)])

    @pl.when(i >= 2)
    def _():
        y_copy(i - 2, False).wait()

    @pl.when(i >= 3)
    def _():
        y_copy(i - 3, True).wait()

    y_cur, y_prev = y_stage.at[i % 2], y_stage.at[(i + 1) % 2]

    @pl.when(i == 0)
    def _():
        for cp in weight_copies:
            cp.start()
        _prompt_tile(0, 0, tiles_per_seq - 1, x_ref, weights, y_cur, y_prev, hp_buf, nc_ref, np_ref,
                     cu_buf, v_buf, weight_copies)
        y_copy(i, False).start()
        for cp in ins:
            cp.start()

    @pl.when((i > 0) & (i < n_prompt))
    def _():
        _prompt_tile(i // tiles_per_seq, i % tiles_per_seq, tiles_per_seq - 1,
                     x_ref, weights, y_cur, y_prev, hp_buf, nc_ref, np_ref, cu_buf, v_buf)
        y_copy(i, False).start()
        y_copy(i - 1, True).start()

    @pl.when(i == n_prompt)
    def _():
        for cp in ins:
            cp.wait()
        _sample_step(weights, xs_buf, sc_buf, sp_buf, ncs_buf, nps_buf, y_prev, hp_buf)
        y_copy(i - 1, True).start()
        for cp in outs:
            cp.start()
        for cp in outs:
            cp.wait()
        y_copy(i - 1, False).wait()
        y_copy(i - 2, True).wait()
        y_copy(i - 1, True).wait()


def _const_spec(shape):
    nd = len(shape)
    return pl.BlockSpec(shape, lambda *_: (0,) * nd, pipeline_mode=pl.Buffered(1))


def kernel(x_prompt, x_sample, state_conv, state_pool, norm_g, w_in, conv_w, pool_w, pool_scale,
           w_out, final_g):
    assert norm_g.shape[0] == 1, "single-layer trunk"
    batch, seq, _ = x_prompt.shape
    n_seq, n_tok, _ = x_sample.shape
    assert seq % SEQ_TILE == 0 and n_tok < POOL_HIST and CONV_HIST <= CONV_PAD
    assert SUB_TILES[0][0] == 0 and sum(n for _, n in SUB_TILES) == SEQ_TILE
    assert all(a + n == b for (a, n), (b, _) in zip(SUB_TILES, SUB_TILES[1:]))
    assert all(n % BF16_TILE_ROWS == 0 and n >= POOL_MAX for _, n in SUB_TILES)
    tiles_per_seq = seq // SEQ_TILE
    n_prompt = batch * tiles_per_seq
    assert n_prompt >= 2, "the output copy schedule drains two tiles in the decode step"

    small = (norm_g, jnp.swapaxes(conv_w, 0, 1), pool_w[0], pool_scale, final_g.reshape(1, D_MODEL))

    def tile(i):
        p = jnp.minimum(i, n_prompt - 1)
        return p // tiles_per_seq, p % tiles_per_seq

    hbm = pl.BlockSpec(memory_space=pl.ANY)
    outs = pl.pallas_call(
        functools.partial(_mixer_kernel, tiles_per_seq=tiles_per_seq),
        grid=(n_prompt + 1,),
        in_specs=[pl.BlockSpec((1, SEQ_TILE, D_MODEL), lambda i: (*tile(i), 0))] + [hbm] * 5
        + [_const_spec(a.shape) for a in small],
        out_specs=[
            hbm,
            pl.BlockSpec((1, batch, CONV_HIST, D_CONV), lambda i: (0, 0, 0, 0)),
            pl.BlockSpec((POOL_HIST, batch, D_POOL), lambda i: (0, 0, 0)),
            hbm, hbm, hbm,
        ],
        out_shape=[
            jax.ShapeDtypeStruct((batch, seq, D_MODEL), _F32),
            jax.ShapeDtypeStruct((1, batch, CONV_HIST, D_CONV), _F32),
            jax.ShapeDtypeStruct((POOL_HIST, batch, D_POOL), _F32),
            jax.ShapeDtypeStruct((n_seq, n_tok, D_MODEL), _F32),
            jax.ShapeDtypeStruct((n_seq, CONV_HIST, D_CONV), _F32),
            jax.ShapeDtypeStruct((POOL_HIST, n_seq, D_POOL), _F32),
        ],
        scratch_shapes=[
            pltpu.VMEM((D_MODEL, D_IN), _F32),
            pltpu.VMEM((D_MODEL, D_MODEL), _F32),
            pltpu.VMEM((2, SEQ_TILE, D_MODEL), _F32),
            pltpu.VMEM((SUB_TILES[-1][1], D_MODEL), _F32),
            pltpu.VMEM((CONV_PAD + SEQ_TILE, D_CONV), _F32),
            pltpu.VMEM((POOL_PAD + SEQ_TILE, D_POOL), _F32),
            pltpu.VMEM((n_tok, n_seq, D_MODEL), _F32),
            pltpu.VMEM((CONV_HIST, n_seq, D_CONV), _F32),
            pltpu.VMEM((POOL_HIST, n_seq, D_POOL), _F32),
            pltpu.VMEM((CONV_HIST, n_seq, D_CONV), _F32),
            pltpu.VMEM((n_tok, n_seq, D_POOL), _F32),
            pltpu.SemaphoreType.DMA((4,)),
            pltpu.SemaphoreType.DMA((4,)),
            pltpu.SemaphoreType.DMA((n_tok + CONV_HIST + 2,)),
            pltpu.SemaphoreType.DMA((n_tok + CONV_HIST + 1,)),
        ],
        compiler_params=pltpu.CompilerParams(
            dimension_semantics=("arbitrary",),
            vmem_limit_bytes=VMEM_LIMIT_BYTES),
        name="mixer_step",
    )(x_prompt, x_sample, state_conv[0], jnp.swapaxes(state_pool[0], 0, 1), w_in[0], w_out[0], *small)
    y_prompt, nc_p, np_p, y_sample, nc_s, np_s = outs
    return (y_prompt, y_sample, nc_p, jnp.swapaxes(np_p, 0, 1)[None], nc_s[None],
            jnp.swapaxes(np_s, 0, 1)[None])
```

```python
import functools

import jax
import jax.numpy as jnp
from jax import lax
from jax.experimental import pallas as pl
from jax.experimental.pallas import tpu as pltpu

D_MODEL = 1024
D_CONV = 512
D_POOL = 512
POOL_WINDOWS = (2, 4, 8, 16)
POOL_GC = D_POOL // len(POOL_WINDOWS)
POOL_MAX = max(POOL_WINDOWS)
POOL_HIST = POOL_MAX - 1
CONV_W = 3
CONV_HIST = CONV_W - 1
D_IN = 4 * D_CONV + 2 * D_POOL
EPS = 1e-6

SEQ_TILE = 1024
SUB_TILES = ((0, 384), (384, 384), (768, 256))
BF16_TILE_ROWS = 16
CONV_PAD = 8
POOL_PAD = 16
VMEM_LIMIT_BYTES = 62 * 1024 * 1024

_F32 = jnp.float32

_COL_B, _COL_C, _COL_U, _COL_ZA = 0, D_CONV, 2 * D_CONV, 3 * D_CONV
_COL_V, _COL_ZB = 4 * D_CONV, 4 * D_CONV + D_POOL
_CONV_PIECES = ((_COL_B, 2), (_COL_U, 1), (_COL_ZA, 1))


def _rmsnorm(x, g):
    ms = jnp.mean(x * x, axis=-1, keepdims=True)
    return (x * lax.rsqrt(ms + EPS)) * g


def _silu(z):
    return z * jax.nn.sigmoid(z)


def _dot(a, b):
    return jnp.dot(a.astype(jnp.bfloat16), b.astype(jnp.bfloat16), preferred_element_type=_F32)


def _pool_project(pooled_groups, pw_ref):
    outs = [_dot(p, pw_ref[gi]) for gi, p in enumerate(pooled_groups)]
    return jnp.concatenate(outs, axis=-1)


def _window_sums(ext):
    assert all(b == 2 * a for a, b in zip((1,) + POOL_WINDOWS, POOL_WINDOWS)) and POOL_PAD >= POOL_MAX
    sums, width, out = ext, 1, []
    for w in POOL_WINDOWS:
        while width < w:
            sums = sums + pltpu.roll(sums, width, axis=0)
            width *= 2
        out.append(sums[POOL_PAD:, 0:POOL_GC])
        sums = sums[:, POOL_GC:]
    return out


def _finish_tile(y_stage, hp_buf, fg_ref):
    r0, n = SUB_TILES[-1]
    y_stage[r0:r0 + n, :] = _rmsnorm(hp_buf[...], fg_ref[...])


def _prompt_tile(b, j, last_j, x_ref, weights, y_cur, y_prev, hp_buf, nc_ref, np_ref, cu_buf, v_buf,
                 weight_copies=None):
    def arrive(k):
        if weight_copies is not None:
            weight_copies[k].wait()

    ng_ref, win_ref, cw_ref, pw_ref, ps_ref, wout_ref, fg_ref = weights
    ts = SEQ_TILE
    head = POOL_MAX

    @pl.when(j == 0)
    def _():
        cu_buf[0:CONV_PAD, :] = jnp.zeros((CONV_PAD, D_CONV), _F32)
        v_buf[0:POOL_PAD, :] = jnp.zeros((POOL_PAD, D_POOL), _F32)

    cw = [cw_ref[k] for k in range(CONV_W)]
    pos = j * ts + 1 + lax.broadcasted_iota(jnp.int32, (head, POOL_GC), 0)

    xs = [x_ref[0, r0:r0 + n, :] for r0, n in SUB_TILES]
    hs = [_rmsnorm(x, ng_ref[...]) for x in xs]
    if weight_copies is None:
        _finish_tile(y_prev, hp_buf, fg_ref)
    arrive(0)
    vs = [_dot(h, win_ref[:, _COL_V:_COL_V + D_POOL]) for h in hs]
    arrive(1)
    zbs = [_dot(h, win_ref[:, _COL_ZB:_COL_ZB + D_POOL]) for h in hs]
    if weight_copies is None:
        projs = [_dot(h, win_ref[:, _COL_B:_COL_ZA + D_CONV]) for h in hs]
        parts = [[proj[:, c:c + D_CONV] for proj in projs] for c in (_COL_B, _COL_C, _COL_U, _COL_ZA)]
    else:
        parts = []
        for k, (col, n_parts) in enumerate(_CONV_PIECES):
            arrive(2 + k)
            piece = [_dot(h, win_ref[:, col:col + n_parts * D_CONV]) for h in hs]
            parts += [[p[:, q * D_CONV:(q + 1) * D_CONV] for p in piece] for q in range(n_parts)]
    b_gs, c_gs, us, z_as = parts

    pooled = []
    for s, v in enumerate(vs):
        r0, n = SUB_TILES[s]
        p0 = POOL_PAD + r0
        v_buf[p0:p0 + n, :] = v
        wsums = _window_sums(v_buf[p0 - POOL_PAD:p0 + n, :])
        groups = []
        for gi, w in enumerate(POOL_WINDOWS):
            acc = wsums[gi]
            if s == 0:
                cnt = jnp.minimum(pos, w).astype(_F32)
                mean = jnp.concatenate([acc[0:head] / cnt, acc[head:] * (1.0 / w)], axis=0)
            else:
                mean = acc * (1.0 / w)
            groups.append(mean - v[:, gi * POOL_GC:(gi + 1) * POOL_GC])
        pooled.append(groups)

    y_as = []
    for s, (r0, n) in enumerate(SUB_TILES):
        cu = c_gs[s] * us[s]
        c0 = CONV_PAD + r0
        cu_buf[c0:c0 + n, :] = cu
        ext = cu_buf[c0 - CONV_PAD:c0 + n, :]
        yc = cw[0] * pltpu.roll(ext, CONV_HIST, axis=0)[CONV_PAD:]
        for k in range(1, CONV_HIST):
            yc = yc + cw[k] * pltpu.roll(ext, CONV_HIST - k, axis=0)[CONV_PAD:]
        yc = yc + cw[CONV_HIST] * cu
        y_as.append(b_gs[s] * yc * _silu(z_as[s]))

    y_bs = [_pool_project(p, pw_ref) * ps_ref[...] * _silu(z_b) for p, z_b in zip(pooled, zbs)]
    arrive(2 + len(_CONV_PIECES))
    mixes = [_dot(jnp.concatenate([y_a, y_b], axis=-1), wout_ref[...]) for y_a, y_b in zip(y_as, y_bs)]
    for (r0, n), x, mix in zip(SUB_TILES[:-1], xs, mixes):
        y_cur[r0:r0 + n, :] = _rmsnorm(x + mix, fg_ref[...])
    hp_buf[...] = xs[-1] + mixes[-1]

    @pl.when(j == last_j)
    def _():
        nc_ref[0, b] = cu_buf[CONV_PAD + ts - CONV_HIST:CONV_PAD + ts, :]

    tail0 = POOL_PAD + ts - POOL_HIST
    n_seq = np_ref.shape[1]

    @pl.when((j == last_j) & (b == 0))
    def _():
        for k in range(POOL_HIST):
            np_ref[k] = jnp.broadcast_to(v_buf[tail0 + k:tail0 + k + 1, :], (n_seq, D_POOL))

    @pl.when((j == last_j) & (b > 0))
    def _():
        mine = lax.broadcasted_iota(jnp.int32, (n_seq, D_POOL), 0) == b
        for k in range(POOL_HIST):
            row = jnp.broadcast_to(v_buf[tail0 + k:tail0 + k + 1, :], (n_seq, D_POOL))
            np_ref[k] = jnp.where(mine, row, np_ref[k])

    cu_buf[0:CONV_PAD, :] = cu_buf[ts:ts + CONV_PAD, :]
    v_buf[0:POOL_PAD, :] = v_buf[ts:ts + POOL_PAD, :]


def _sample_copies(xs_hbm, sc_hbm, sp_hbm, ys_hbm, ncs_hbm, nps_hbm,
                   xs_buf, sc_buf, sp_buf, ncs_buf, nps_buf, in_sems, out_sems):
    n_tok, n_cs = xs_buf.shape[0], sc_buf.shape[0]
    keep = POOL_HIST - n_tok
    ins = ([pltpu.make_async_copy(xs_hbm.at[:, t, :], xs_buf.at[t], in_sems.at[t]) for t in range(n_tok)]
           + [pltpu.make_async_copy(sc_hbm.at[:, k, :], sc_buf.at[k], in_sems.at[n_tok + k])
              for k in range(n_cs)]
           + [pltpu.make_async_copy(sp_hbm, sp_buf, in_sems.at[n_tok + n_cs]),
              pltpu.make_async_copy(sp_hbm.at[pl.ds(n_tok, keep)], nps_hbm.at[pl.ds(0, keep)],
                                    in_sems.at[n_tok + n_cs + 1])])
    outs = ([pltpu.make_async_copy(xs_buf.at[t], ys_hbm.at[:, t, :], out_sems.at[t]) for t in range(n_tok)]
            + [pltpu.make_async_copy(ncs_buf.at[k], ncs_hbm.at[:, k, :], out_sems.at[n_tok + k])
               for k in range(n_cs)]
            + [pltpu.make_async_copy(nps_buf, nps_hbm.at[pl.ds(keep, n_tok)], out_sems.at[n_tok + n_cs])])
    return ins, outs


def _sample_step(weights, xs_buf, sc_buf, sp_buf, ncs_buf, nps_buf, y_prev, hp_buf):
    ng_ref, win_ref, cw_ref, pw_ref, ps_ref, wout_ref, fg_ref = weights
    n_tok, n_seq, _ = xs_buf.shape

    def tok(a, t):
        return a[t * n_seq:(t + 1) * n_seq, :]

    x = jnp.concatenate([xs_buf[t] for t in range(n_tok)], axis=0)
    h = _rmsnorm(x, ng_ref[...])
    _finish_tile(y_prev, hp_buf, fg_ref)
    v = _dot(h, win_ref[:, _COL_V:_COL_V + D_POOL])
    z_b = _dot(h, win_ref[:, _COL_ZB:_COL_ZB + D_POOL])
    proj = _dot(h, win_ref[:, _COL_B:_COL_ZA + D_CONV])

    vp = [sp_buf[k] for k in range(POOL_HIST)] + [tok(v, t) for t in range(n_tok)]
    for t in range(n_tok):
        nps_buf[t] = tok(v, t)
    pooled = []
    for gi, w in enumerate(POOL_WINDOWS):
        lo = gi * POOL_GC
        rows = []
        for t in range(n_tok):
            vg = vp[POOL_HIST + t][:, lo:lo + POOL_GC]
            acc = vg
            for k in range(1, w):
                acc = acc + vp[POOL_HIST + t - k][:, lo:lo + POOL_GC]
            rows.append(acc * (1.0 / w) - vg)
        pooled.append(jnp.concatenate(rows, axis=0))
    y_b = _pool_project(pooled, pw_ref) * ps_ref[...] * _silu(z_b)

    b_g, c_g, u, z_a = (proj[:, c:c + D_CONV] for c in (_COL_B, _COL_C, _COL_U, _COL_ZA))
    cu = c_g * u
    cw = [cw_ref[k] for k in range(CONV_W)]
    up = [sc_buf[k] for k in range(CONV_HIST)] + [tok(cu, t) for t in range(n_tok)]
    for k in range(CONV_HIST):
        ncs_buf[k] = up[n_tok + k]
    yc = []
    for t in range(n_tok):
        acc = cw[0] * up[t]
        for k in range(1, CONV_W):
            acc = acc + cw[k] * up[t + k]
        yc.append(acc)
    y_a = b_g * jnp.concatenate(yc, axis=0) * _silu(z_a)

    ycat = jnp.concatenate([y_a, y_b], axis=-1)
    y = _rmsnorm(x + _dot(ycat, wout_ref[...]), fg_ref[...])
    for t in range(n_tok):
        xs_buf[t] = tok(y, t)


def _mixer_kernel(x_ref, xs_hbm, sc_hbm, sp_hbm, win_hbm, wout_hbm, ng_ref, cw_ref, pw_ref, ps_ref, fg_ref,
                  y_hbm, nc_ref, np_ref, ys_hbm, ncs_hbm, nps_hbm,
                  win_ref, wout_ref, y_stage, hp_buf, cu_buf, v_buf, xs_buf, sc_buf, sp_buf, ncs_buf, nps_buf,
                  w_sems, y_sems, in_sems, out_sems, *, tiles_per_seq):
    i = pl.program_id(0)
    n_prompt = pl.num_programs(0) - 1
    weights = (ng_ref, win_ref, cw_ref, pw_ref, ps_ref, wout_ref, fg_ref)
    w_cols = ((_COL_V, D_POOL), (_COL_ZB, D_POOL)) + tuple((c, n * D_CONV) for c, n in _CONV_PIECES)
    weight_copies = [pltpu.make_async_copy(win_hbm.at[:, pl.ds(c, n)], win_ref.at[:, pl.ds(c, n)],
                                           w_sems.at[k]) for k, (c, n) in enumerate(w_cols)]
    weight_copies.append(pltpu.make_async_copy(wout_hbm, wout_ref, w_sems.at[len(w_cols)]))
    ins, outs = _sample_copies(xs_hbm, sc_hbm, sp_hbm, ys_hbm, ncs_hbm, nps_hbm,
                               xs_buf, sc_buf, sp_buf, ncs_buf, nps_buf, in_sems, out_sems)

    tail_r0, tail_n = SUB_TILES[-1]

    def y_copy(p, tail):
        slot = p % 2
        r0, n = (tail_r0, tail_n) if tail else (0, tail_r0)
        return pltpu.make_async_copy(
            y_stage.at[slot, pl.ds(r0, n)],
            y_hbm.at[p // tiles_per_seq, pl.ds((p % tiles_per_seq) * SEQ_TILE + r0, n)],
            y_sems.at[2 * slot + int(tail)])

    @pl.when(i >= 2)
    def _():
        y_copy(i - 2, False).wait()

    @pl.when(i >= 3)
    def _():
        y_copy(i - 3, True).wait()

    y_cur, y_prev = y_stage.at[i % 2], y_stage.at[(i + 1) % 2]

    @pl.when(i == 0)
    def _():
        for cp in weight_copies:
            cp.start()
        _prompt_tile(0, 0, tiles_per_seq - 1, x_ref, weights, y_cur, y_prev, hp_buf, nc_ref, np_ref,
                     cu_buf, v_buf, weight_copies)
        y_copy(i, False).start()
        for cp in ins:
            cp.start()

    @pl.when((i > 0) & (i < n_prompt))
    def _():
        _prompt_tile(i // tiles_per_seq, i % tiles_per_seq, tiles_per_seq - 1,
                     x_ref, weights, y_cur, y_prev, hp_buf, nc_ref, np_ref, cu_buf, v_buf)
        y_copy(i, False).start()
        y_copy(i - 1, True).start()

    @pl.when(i == n_prompt)
    def _():
        for cp in ins:
            cp.wait()
        _sample_step(weights, xs_buf, sc_buf, sp_buf, ncs_buf, nps_buf, y_prev, hp_buf)
        y_copy(i - 1, True).start()
        for cp in outs:
            cp.start()
        for cp in outs:
            cp.wait()
        y_copy(i - 1, False).wait()
        y_copy(i - 2, True).wait()
        y_copy(i - 1, True).wait()


def _const_spec(shape):
    nd = len(shape)
    return pl.BlockSpec(shape, lambda *_: (0,) * nd, pipeline_mode=pl.Buffered(1))


def kernel(x_prompt, x_sample, state_conv, state_pool, norm_g, w_in, conv_w, pool_w, pool_scale,
           w_out, final_g):
    assert norm_g.shape[0] == 1, "single-layer trunk"
    batch, seq, _ = x_prompt.shape
    n_seq, n_tok, _ = x_sample.shape
    assert seq % SEQ_TILE == 0 and n_tok < POOL_HIST and CONV_HIST <= CONV_PAD
    assert SUB_TILES[0][0] == 0 and sum(n for _, n in SUB_TILES) == SEQ_TILE
    assert all(a + n == b for (a, n), (b, _) in zip(SUB_TILES, SUB_TILES[1:]))
    assert all(n % BF16_TILE_ROWS == 0 and n >= POOL_MAX for _, n in SUB_TILES)
    tiles_per_seq = seq // SEQ_TILE
    n_prompt = batch * tiles_per_seq
    assert n_prompt >= 2, "the output copy schedule drains two tiles in the decode step"

    small = (norm_g, jnp.swapaxes(conv_w, 0, 1), pool_w[0], pool_scale, final_g.reshape(1, D_MODEL))

    def tile(i):
        p = jnp.minimum(i, n_prompt - 1)
        return p // tiles_per_seq, p % tiles_per_seq

    hbm = pl.BlockSpec(memory_space=pl.ANY)
    outs = pl.pallas_call(
        functools.partial(_mixer_kernel, tiles_per_seq=tiles_per_seq),
        grid=(n_prompt + 1,),
        in_specs=[pl.BlockSpec((1, SEQ_TILE, D_MODEL), lambda i: (*tile(i), 0))] + [hbm] * 5
        + [_const_spec(a.shape) for a in small],
        out_specs=[
            hbm,
            pl.BlockSpec((1, batch, CONV_HIST, D_CONV), lambda i: (0, 0, 0, 0)),
            pl.BlockSpec((POOL_HIST, batch, D_POOL), lambda i: (0, 0, 0)),
            hbm, hbm, hbm,
        ],
        out_shape=[
            jax.ShapeDtypeStruct((batch, seq, D_MODEL), _F32),
            jax.ShapeDtypeStruct((1, batch, CONV_HIST, D_CONV), _F32),
            jax.ShapeDtypeStruct((POOL_HIST, batch, D_POOL), _F32),
            jax.ShapeDtypeStruct((n_seq, n_tok, D_MODEL), _F32),
            jax.ShapeDtypeStruct((n_seq, CONV_HIST, D_CONV), _F32),
            jax.ShapeDtypeStruct((POOL_HIST, n_seq, D_POOL), _F32),
        ],
        scratch_shapes=[
            pltpu.VMEM((D_MODEL, D_IN), _F32),
            pltpu.VMEM((D_MODEL, D_MODEL), _F32),
            pltpu.VMEM((2, SEQ_TILE, D_MODEL), _F32),
            pltpu.VMEM((SUB_TILES[-1][1], D_MODEL), _F32),
            pltpu.VMEM((CONV_PAD + SEQ_TILE, D_CONV), _F32),
            pltpu.VMEM((POOL_PAD + SEQ_TILE, D_POOL), _F32),
            pltpu.VMEM((n_tok, n_seq, D_MODEL), _F32),
            pltpu.VMEM((CONV_HIST, n_seq, D_CONV), _F32),
            pltpu.VMEM((POOL_HIST, n_seq, D_POOL), _F32),
            pltpu.VMEM((CONV_HIST, n_seq, D_CONV), _F32),
            pltpu.VMEM((n_tok, n_seq, D_POOL), _F32),
            pltpu.SemaphoreType.DMA((3 + len(_CONV_PIECES),)),
            pltpu.SemaphoreType.DMA((4,)),
            pltpu.SemaphoreType.DMA((n_tok + CONV_HIST + 2,)),
            pltpu.SemaphoreType.DMA((n_tok + CONV_HIST + 1,)),
        ],
        compiler_params=pltpu.CompilerParams(
            dimension_semantics=("arbitrary",),
            vmem_limit_bytes=VMEM_LIMIT_BYTES),
        name="mixer_step",
    )(x_prompt, x_sample, state_conv[0], jnp.swapaxes(state_pool[0], 0, 1), w_in[0], w_out[0], *small)
    y_prompt, nc_p, np_p, y_sample, nc_s, np_s = outs
    return (y_prompt, y_sample, nc_p, jnp.swapaxes(np_p, 0, 1)[None], nc_s[None],
            jnp.swapaxes(np_s, 0, 1)[None])
```
